```python
import math
import jax, jax.numpy as jnp
from jax import lax
import numpy as np

D_MODEL = 1024
BATCH = 4
SEQ = 4096
DEPTH = 2

CHUNK = 64
HEAD_DIM = 64
D_CONV = D_MODEL // 2
N_SB_HEADS = 8
D_SB = N_SB_HEADS * HEAD_DIM
D_MIX = D_CONV + D_SB
CONV_WIDTH = 3
PLE_DIM = 256
Q_BLOCK = 128
N_IN = 4 * D_CONV + 4 * D_SB
EPS = 1e-6

kernel_name = "hybrid_shortconv_stickbreaking_ple"


def rmsnorm(x, g):
    xf = x.astype(jnp.float32)
    y = xf * lax.rsqrt(jnp.mean(xf * xf, axis=-1, keepdims=True) + EPS)
    return (y * g.astype(jnp.float32)).astype(x.dtype)


def group_rmsnorm(y, g, group):
    shp = y.shape
    yf = y.astype(jnp.float32).reshape(shp[:-1] + (shp[-1] // group, group))
    yf = yf * lax.rsqrt(jnp.mean(yf * yf, axis=-1, keepdims=True) + EPS)
    return (yf.reshape(shp) * g.astype(jnp.float32)).astype(y.dtype)


def causal_dwconv(u, w, b):
    s = u.shape[1]
    up = jnp.pad(u, ((0, 0), (CONV_WIDTH - 1, 0), (0, 0)))
    y = b
    for j in range(CONV_WIDTH):
        y = y + up[:, j:j + s, :] * w[j]
    return y


def stick_breaking_block(q_blk, k_pre, v_pre, t0):
    dh = q_blk.shape[-1]
    z = jnp.einsum('bqhd,bkhd->bhqk', q_blk.astype(jnp.float32), k_pre.astype(jnp.float32)) / math.sqrt(dh)
    qb, kl = q_blk.shape[1], k_pre.shape[1]
    t_idx = t0 + jnp.arange(qb)[:, None]
    s_idx = jnp.arange(kl)[None, :]
    mask = s_idx < t_idx
    log_1m = jnp.where(mask, jax.nn.log_sigmoid(-z), 0.0)
    rem = lax.cumsum(log_1m, axis=3, reverse=True) - log_1m
    a = jnp.where(mask, jnp.exp(jax.nn.log_sigmoid(z) + rem), 0.0)
    out = jnp.einsum('bhqk,bkhd->bqhd', a, v_pre.astype(jnp.float32))
    return out.astype(q_blk.dtype)


def stick_breaking_attention(q, k, v):
    s = q.shape[1]
    outs = []
    for blk in range(s // Q_BLOCK):
        t0 = blk * Q_BLOCK
        kend = t0 + Q_BLOCK
        outs.append(stick_breaking_block(q[:, t0:kend], k[:, :kend], v[:, :kend], t0))
    return jnp.concatenate(outs, axis=1)


def setup_inputs(seed: int = 0) -> dict:
    key = jax.random.key(seed)
    ks = jax.random.split(key, 14)
    f32 = jnp.float32
    x = jax.random.normal(ks[0], (BATCH, SEQ, D_MODEL), f32)
    p = jax.random.normal(ks[1], (DEPTH, BATCH, SEQ, PLE_DIM), f32)
    norm_g = 1.0 + 0.02 * jax.random.normal(ks[2], (DEPTH, D_MODEL), f32)
    w_in = jax.random.normal(ks[3], (DEPTH, D_MODEL, N_IN), f32) * D_MODEL ** -0.5
    conv_w = jax.random.normal(ks[4], (DEPTH, CONV_WIDTH, D_CONV), f32) * CONV_WIDTH ** -0.5
    conv_b = 0.02 * jax.random.normal(ks[5], (DEPTH, D_CONV), f32)
    branch_g = 1.0 + 0.02 * jax.random.normal(ks[6], (DEPTH, D_MIX), f32)
    w_out = jax.random.normal(ks[7], (DEPTH, D_MIX, D_MODEL), f32) * D_MIX ** -0.5
    ple_norm_g = 1.0 + 0.02 * jax.random.normal(ks[8], (DEPTH, D_MODEL), f32)
    w_pg = jax.random.normal(ks[9], (DEPTH, D_MODEL, D_MODEL), f32) * D_MODEL ** -0.5
    b_pg = 0.02 * jax.random.normal(ks[10], (DEPTH, D_MODEL), f32)
    w_pe = jax.random.normal(ks[11], (DEPTH, PLE_DIM, D_MODEL), f32) * PLE_DIM ** -0.5
    final_g = 1.0 + 0.02 * jax.random.normal(ks[12], (D_MODEL,), f32)
    return {"x": x, "p": p, "norm_g": norm_g, "w_in": w_in, "conv_w": conv_w,
            "conv_b": conv_b, "branch_g": branch_g, "w_out": w_out,
            "ple_norm_g": ple_norm_g, "w_pg": w_pg, "b_pg": b_pg, "w_pe": w_pe,
            "final_g": final_g}


def reference(x, p, norm_g, w_in, conv_w, conv_b, branch_g, w_out,
              ple_norm_g, w_pg, b_pg, w_pe, final_g):
    bsz, s, _ = x.shape
    for i in range(DEPTH):
        h = rmsnorm(x, norm_g[i])
        proj = h @ w_in[i]
        c_b, c_c, c_h, c_z, q, k, v, a_z = jnp.split(
            proj, [D_CONV, 2 * D_CONV, 3 * D_CONV, 4 * D_CONV,
                   4 * D_CONV + D_SB, 4 * D_CONV + 2 * D_SB, 4 * D_CONV + 3 * D_SB], axis=-1)
        y_c = c_b * causal_dwconv(c_c * c_h, conv_w[i], conv_b[i])
        qh = q.reshape(bsz, s, N_SB_HEADS, HEAD_DIM)
        kh = k.reshape(bsz, s, N_SB_HEADS, HEAD_DIM)
        vh = v.reshape(bsz, s, N_SB_HEADS, HEAD_DIM)
        y_a = stick_breaking_attention(qh, kh, vh).reshape(bsz, s, D_SB)
        y = group_rmsnorm(jnp.concatenate([y_c, y_a], axis=-1), branch_g[i], HEAD_DIM)
        y = y * jax.nn.silu(jnp.concatenate([c_z, a_z], axis=-1))
        x = x + y @ w_out[i]
        gate = jax.nn.sigmoid(rmsnorm(x, ple_norm_g[i]) @ w_pg[i] + b_pg[i])
        x = x + gate * (p[i] @ w_pe[i])
    return rmsnorm(x, final_g)
```

```python
import functools

import jax
import jax.numpy as jnp
from jax import lax
from jax.experimental import pallas as pl
from jax.experimental.pallas import tpu as pltpu

HEAD_DIM = 64
LANES = 128
CONV_WIDTH = 3
EPS = 1e-6

ROW_TILE = 512
ATTN_TILE = 256
CARRY_ROWS = 8
VMEM_LIMIT = 48 * 1024 * 1024

f32 = jnp.float32
bf16 = jnp.bfloat16


def _rmsnorm(x, g):
    return x * lax.rsqrt(jnp.mean(x * x, axis=-1, keepdims=True) + EPS) * g


def _sigmoid(x):
    return 1.0 / (1.0 + jnp.exp(-x))


def _pair_group_norm(y, lo_half):
    sq = y * y
    s_lo = jnp.sum(jnp.where(lo_half, sq, 0.0), axis=-1, keepdims=True)
    s_hi = jnp.sum(jnp.where(lo_half, 0.0, sq), axis=-1, keepdims=True)
    inv_lo = lax.rsqrt(s_lo * (1.0 / HEAD_DIM) + EPS)
    inv_hi = lax.rsqrt(s_hi * (1.0 / HEAD_DIM) + EPS)
    return y * jnp.where(lo_half, inv_lo, inv_hi)


def _in_proj_kernel(x_ref, g_ref, w_ref, cw_ref, cb_ref, bg_ref,
                    yc_ref, q_ref, k_ref, v_ref, az_ref, u_scr, *, d_conv, d_sb):
    rows = x_ref.shape[1]
    h = _rmsnorm(x_ref[0], g_ref[...]).astype(bf16)

    def proj(col0, width):
        return jnp.dot(h, w_ref[:, col0:col0 + width], preferred_element_type=f32)

    c_b = proj(0, d_conv)
    u = proj(d_conv, d_conv) * proj(2 * d_conv, d_conv)
    c_z = proj(3 * d_conv, d_conv)
    base = 4 * d_conv
    q_ref[0] = (proj(base, d_sb) * (HEAD_DIM ** -0.5)).astype(bf16)
    k_ref[0] = proj(base + d_sb, d_sb).astype(bf16)
    v_ref[0] = proj(base + 2 * d_sb, d_sb).astype(bf16)
    az_ref[0] = proj(base + 3 * d_sb, d_sb)

    @pl.when(pl.program_id(1) == 0)
    def _():
        u_scr[0:CARRY_ROWS, :] = jnp.zeros((CARRY_ROWS, d_conv), f32)

    u_scr[CARRY_ROWS:CARRY_ROWS + rows, :] = u
    conv = cb_ref[...]
    for j in range(CONV_WIDTH):
        back = CONV_WIDTH - 1 - j
        conv = conv + u_scr[CARRY_ROWS - back:CARRY_ROWS - back + rows, :] * cw_ref[j:j + 1, :]
    u_scr[0:CARRY_ROWS, :] = u_scr[rows:rows + CARRY_ROWS, :]

    y_c = c_b * conv
    gate = c_z * _sigmoid(c_z)
    lo_half = lax.broadcasted_iota(jnp.int32, (1, LANES), 1) < HEAD_DIM
    for j in range(d_conv // LANES):
        sl = slice(j * LANES, (j + 1) * LANES)
        y = _pair_group_norm(y_c[:, sl], lo_half) * bg_ref[:, sl]
        yc_ref[0, :, sl] = (y * gate[:, sl]).astype(bf16)


def _in_proj(x, norm_g, w_in, conv_w, conv_b, bg_conv, d_conv, d_sb):
    bsz, s, d = x.shape
    n_in = w_in.shape[1]
    row_spec = lambda width: pl.BlockSpec((1, ROW_TILE, width), lambda b, t: (b, t, 0))
    const = lambda shape: pl.BlockSpec(shape, lambda b, t: (0, 0), pipeline_mode=pl.Buffered(1))
    return pl.pallas_call(
        functools.partial(_in_proj_kernel, d_conv=d_conv, d_sb=d_sb),
        grid=(bsz, s // ROW_TILE),
        in_specs=[row_spec(d), const((1, d)), const((d, n_in)),
                  const((CONV_WIDTH, d_conv)), const((1, d_conv)), const((1, d_conv))],
        out_specs=[row_spec(d_conv), row_spec(d_sb), row_spec(d_sb), row_spec(d_sb), row_spec(d_sb)],
        out_shape=[jax.ShapeDtypeStruct((bsz, s, d_conv), bf16),
                   jax.ShapeDtypeStruct((bsz, s, d_sb), bf16),
                   jax.ShapeDtypeStruct((bsz, s, d_sb), bf16),
                   jax.ShapeDtypeStruct((bsz, s, d_sb), bf16),
                   jax.ShapeDtypeStruct((bsz, s, d_sb), f32)],
        scratch_shapes=[pltpu.VMEM((ROW_TILE + CARRY_ROWS, d_conv), f32)],
        compiler_params=pltpu.CompilerParams(
            dimension_semantics=("arbitrary", "arbitrary"), vmem_limit_bytes=VMEM_LIMIT),
        name="in_proj",
    )(x, norm_g, w_in, conv_w, conv_b, bg_conv)


def _attn_kernel(q_ref, k_ref, v_ref, az_ref, g_ref, o_ref):
    t = ATTN_TILE
    i = pl.program_id(2)
    lo_half = lax.broadcasted_iota(jnp.int32, (1, LANES), 1) < HEAD_DIM
    key_pos = lax.broadcasted_iota(jnp.int32, (t, t), 1)
    query_pos = lax.broadcasted_iota(jnp.int32, (t, t), 0)
    causal = key_pos < query_pos
    later_keys = (query_pos > key_pos).astype(bf16)
    q = q_ref[0]

    def tile(q_h, j, c, acc, diagonal):
        start = pl.multiple_of(j * t, t)
        k_t = k_ref[0, pl.ds(start, t), :]
        v_t = v_ref[0, pl.ds(start, t), :]
        z = lax.dot_general(q_h, k_t, (((1,), (1,)), ((), ())), preferred_element_type=f32)
        l = jnp.log1p(jnp.exp(-jnp.abs(z)))
        sp_pos = jnp.maximum(z, 0.0) + l
        if diagonal:
            sp_neg = jnp.maximum(-z, 0.0) + l
            sp_pos = jnp.where(causal, sp_pos, 0.0)
        else:
            sp_neg = sp_pos - z
        hi = sp_pos.astype(bf16)
        lo = (sp_pos - hi.astype(f32)).astype(bf16)
        suffix = (jnp.dot(hi, later_keys, preferred_element_type=f32)
                  + jnp.dot(lo, later_keys, preferred_element_type=f32))
        a = jnp.exp(-(sp_neg + suffix + c))
        if diagonal:
            a = jnp.where(causal, a, 0.0)
        acc = acc + jnp.dot(a.astype(bf16), v_t, preferred_element_type=f32)
        c = c + jnp.sum(sp_pos, axis=-1, keepdims=True)
        return c, acc

    def head(use_lo):
        q_h = jnp.where(lo_half == use_lo, q, jnp.zeros_like(q))
        c, acc = tile(q_h, i, jnp.zeros((t, 1), f32), jnp.zeros((t, LANES), f32), True)

        def body(n, carry):
            return tile(q_h, i - 1 - n, carry[0], carry[1], False)

        return lax.fori_loop(0, i, body, (c, acc))[1]

    out = jnp.where(lo_half, head(True), head(False))
    y = _pair_group_norm(out, lo_half) * g_ref[...]
    a_z = az_ref[0]
    o_ref[0] = (y * (a_z * _sigmoid(a_z))).astype(bf16)


def _attention(q, k, v, a_z, bg_attn):
    bsz, s, d_sb = q.shape
    pairs = d_sb // LANES
    tile_spec = pl.BlockSpec((1, ATTN_TILE, LANES), lambda b, p, i: (b, i, p))
    seq_spec = pl.BlockSpec((1, s, LANES), lambda b, p, i: (b, 0, p))
    return pl.pallas_call(
        _attn_kernel,
        grid=(bsz, pairs, s // ATTN_TILE),
        in_specs=[tile_spec, seq_spec, seq_spec, tile_spec,
                  pl.BlockSpec((1, LANES), lambda b, p, i: (0, p))],
        out_specs=tile_spec,
        out_shape=jax.ShapeDtypeStruct((bsz, s, d_sb), bf16),
        compiler_params=pltpu.CompilerParams(
            dimension_semantics=("arbitrary", "arbitrary", "arbitrary"),
            vmem_limit_bytes=VMEM_LIMIT),
        name="attention",
    )(q, k, v, a_z, bg_attn)


def _out_proj_kernel(x_ref, yc_ref, ya_ref, p_ref, wo_ref, pg_ref, wpg_ref, bpg_ref, wpe_ref,
                     fg_ref, o_ref, *, d_conv, final):
    x1 = (x_ref[...]
          + jnp.dot(yc_ref[...], wo_ref[0:d_conv, :], preferred_element_type=f32)
          + jnp.dot(ya_ref[...], wo_ref[d_conv:, :], preferred_element_type=f32))
    h = _rmsnorm(x1, pg_ref[...]).astype(bf16)
    gate = _sigmoid(jnp.dot(h, wpg_ref[...], preferred_element_type=f32) + bpg_ref[...])
    pe = jnp.dot(p_ref[...].astype(bf16), wpe_ref[...], preferred_element_type=f32)
    x2 = x1 + gate * pe
    o_ref[...] = _rmsnorm(x2, fg_ref[...]) if final else x2


def _out_proj(x, y_c, y_a, p, w_out, ple_g, w_pg, b_pg, w_pe, final_g, final):
    n, d = x.shape
    d_conv, d_sb, ple = y_c.shape[1], y_a.shape[1], p.shape[1]
    row_spec = lambda width: pl.BlockSpec((ROW_TILE, width), lambda t: (t, 0))
    const = lambda shape: pl.BlockSpec(shape, lambda t: (0, 0), pipeline_mode=pl.Buffered(1))
    return pl.pallas_call(
        functools.partial(_out_proj_kernel, d_conv=d_conv, final=final),
        grid=(n // ROW_TILE,),
        in_specs=[row_spec(d), row_spec(d_conv), row_spec(d_sb), row_spec(ple),
                  const((d_conv + d_sb, d)), const((1, d)), const((d, d)), const((1, d)),
                  const((ple, d)), const((1, d))],
        out_specs=row_spec(d),
        out_shape=jax.ShapeDtypeStruct((n, d), f32),
        compiler_params=pltpu.CompilerParams(
            dimension_semantics=("arbitrary",), vmem_limit_bytes=VMEM_LIMIT),
        name="out_proj",
    )(x, y_c, y_a, p, w_out, ple_g, w_pg, b_pg, w_pe, final_g)


def kernel(x, p, norm_g, w_in, conv_w, conv_b, branch_g, w_out, ple_norm_g, w_pg, b_pg, w_pe, final_g):
    depth = w_in.shape[0]
    bsz, s, d = x.shape
    d_conv = conv_w.shape[2]
    d_sb = branch_g.shape[1] - d_conv
    assert s % ROW_TILE == 0 and s % ATTN_TILE == 0 and d_sb % LANES == 0 and d_conv % LANES == 0
    row = lambda vec: vec.reshape(1, -1)
    for i in range(depth):
        y_c, q, k, v, a_z = _in_proj(
            x, row(norm_g[i]), w_in[i].astype(bf16), conv_w[i], row(conv_b[i]),
            row(branch_g[i, :d_conv]), d_conv, d_sb)
        y_a = _attention(q, k, v, a_z, row(branch_g[i, d_conv:]))
        x = _out_proj(
            x.reshape(bsz * s, d), y_c.reshape(bsz * s, d_conv), y_a.reshape(bsz * s, d_sb),
            p[i].reshape(bsz * s, -1), w_out[i].astype(bf16), row(ple_norm_g[i]),
            w_pg[i].astype(bf16), row(b_pg[i]), w_pe[i].astype(bf16), row(final_g),
            final=(i == depth - 1)).reshape(bsz, s, d)
    return x
```

```python
import functools

import jax
import jax.numpy as jnp
from jax import lax
from jax.experimental import pallas as pl
from jax.experimental.pallas import tpu as pltpu

HEAD_DIM = 64
LANES = 128
CONV_WIDTH = 3
EPS = 1e-6

ROW_TILE = 512
ATTN_TILE = 256
ATTN_BATCH = 2
CARRY_ROWS = 8
VMEM_LIMIT = 48 * 1024 * 1024
LOG2_E = 1.4426950408889634

f32 = jnp.float32
bf16 = jnp.bfloat16


def _rmsnorm(x, g):
    return x * lax.rsqrt(jnp.mean(x * x, axis=-1, keepdims=True) + EPS) * g


def _sigmoid(x):
    return 1.0 / (1.0 + jnp.exp(-x))


def _pair_group_norm(y, lo_half):
    sq = y * y
    s_lo = jnp.sum(jnp.where(lo_half, sq, 0.0), axis=-1, keepdims=True)
    s_hi = jnp.sum(jnp.where(lo_half, 0.0, sq), axis=-1, keepdims=True)
    inv_lo = lax.rsqrt(s_lo * (1.0 / HEAD_DIM) + EPS)
    inv_hi = lax.rsqrt(s_hi * (1.0 / HEAD_DIM) + EPS)
    return y * jnp.where(lo_half, inv_lo, inv_hi)


def _in_proj_kernel(x_ref, g_ref, w_ref, cw_ref, cb_ref, bg_ref,
                    yc_ref, q_ref, k_ref, v_ref, az_ref, u_scr, *, d_conv, d_sb):
    rows = x_ref.shape[1]
    h = _rmsnorm(x_ref[0], g_ref[...]).astype(bf16)

    def proj(col0, width):
        return jnp.dot(h, w_ref[:, col0:col0 + width], preferred_element_type=f32)

    c_b = proj(0, d_conv)
    u = proj(d_conv, d_conv) * proj(2 * d_conv, d_conv)
    c_z = proj(3 * d_conv, d_conv)
    base = 4 * d_conv
    q_ref[0] = (proj(base, d_sb) * (HEAD_DIM ** -0.5)).astype(bf16)
    k_ref[0] = proj(base + d_sb, d_sb).astype(bf16)
    v_ref[0] = proj(base + 2 * d_sb, d_sb).astype(bf16)
    az_ref[0] = proj(base + 3 * d_sb, d_sb)

    @pl.when(pl.program_id(1) == 0)
    def _():
        u_scr[0:CARRY_ROWS, :] = jnp.zeros((CARRY_ROWS, d_conv), f32)

    u_scr[CARRY_ROWS:CARRY_ROWS + rows, :] = u
    conv = cb_ref[...]
    for j in range(CONV_WIDTH):
        back = CONV_WIDTH - 1 - j
        conv = conv + u_scr[CARRY_ROWS - back:CARRY_ROWS - back + rows, :] * cw_ref[j:j + 1, :]
    u_scr[0:CARRY_ROWS, :] = u_scr[rows:rows + CARRY_ROWS, :]

    y_c = c_b * conv
    gate = c_z * _sigmoid(c_z)
    lo_half = lax.broadcasted_iota(jnp.int32, (1, LANES), 1) < HEAD_DIM
    for j in range(d_conv // LANES):
        sl = slice(j * LANES, (j + 1) * LANES)
        y = _pair_group_norm(y_c[:, sl], lo_half) * bg_ref[:, sl]
        yc_ref[0, :, sl] = (y * gate[:, sl]).astype(bf16)


def _in_proj(x, norm_g, w_in, conv_w, conv_b, bg_conv, d_conv, d_sb):
    bsz, s, d = x.shape
    n_in = w_in.shape[1]
    row_spec = lambda width: pl.BlockSpec((1, ROW_TILE, width), lambda b, t: (b, t, 0))
    const = lambda shape: pl.BlockSpec(shape, lambda b, t: (0, 0), pipeline_mode=pl.Buffered(1))
    return pl.pallas_call(
        functools.partial(_in_proj_kernel, d_conv=d_conv, d_sb=d_sb),
        grid=(bsz, s // ROW_TILE),
        in_specs=[row_spec(d), const((1, d)), const((d, n_in)),
                  const((CONV_WIDTH, d_conv)), const((1, d_conv)), const((1, d_conv))],
        out_specs=[row_spec(d_conv), row_spec(d_sb), row_spec(d_sb), row_spec(d_sb), row_spec(d_sb)],
        out_shape=[jax.ShapeDtypeStruct((bsz, s, d_conv), bf16),
                   jax.ShapeDtypeStruct((bsz, s, d_sb), bf16),
                   jax.ShapeDtypeStruct((bsz, s, d_sb), bf16),
                   jax.ShapeDtypeStruct((bsz, s, d_sb), bf16),
                   jax.ShapeDtypeStruct((bsz, s, d_sb), f32)],
        scratch_shapes=[pltpu.VMEM((ROW_TILE + CARRY_ROWS, d_conv), f32)],
        compiler_params=pltpu.CompilerParams(
            dimension_semantics=("arbitrary", "arbitrary"), vmem_limit_bytes=VMEM_LIMIT),
        name="in_proj",
    )(x, norm_g, w_in, conv_w, conv_b, bg_conv)


MASKED = -1e30


def _attn_kernel(q_ref, k_ref, v_ref, az_ref, g_ref, o_ref, z_scr, d_scr, c_scr, acc_scr):
    t = ATTN_TILE
    i = pl.program_id(2)
    lo_half = lax.broadcasted_iota(jnp.int32, (1, LANES), 1) < HEAD_DIM
    key_pos = lax.broadcasted_iota(jnp.int32, (t, t), 1)
    query_pos = lax.broadcasted_iota(jnp.int32, (t, t), 0)
    causal = key_pos < query_pos
    keys_from = (query_pos >= key_pos).astype(bf16)
    streams = []
    for b in range(q_ref.shape[0]):
        q = q_ref[b]
        zero = jnp.zeros_like(q)
        streams.append((b, jnp.where(lo_half, q, zero)))
        streams.append((b, jnp.where(lo_half, zero, q)))

    def rows_of_tile(m):
        return pl.ds(pl.multiple_of(jnp.maximum(i - m, 0) * t, t), t)

    def scores(m):
        for n, (b, q_h) in enumerate(streams):
            z_scr[n] = lax.dot_general(q_h, k_ref[b, rows_of_tile(m), :], (((1,), (1,)), ((), ())),
                                       preferred_element_type=f32)

    def decay(diagonal):
        for n in range(len(streams)):
            z = z_scr[n]
            sp = jnp.maximum(z, 0.0) + jnp.log(1.0 + jnp.exp2(jnp.abs(z) * -LOG2_E))
            if diagonal:
                sp = jnp.where(causal, sp, 0.0)
            hi = sp.astype(bf16)
            lo = (sp - hi.astype(f32)).astype(bf16)
            suffix = (jnp.dot(hi, keys_from, preferred_element_type=f32)
                      + jnp.dot(lo, keys_from, preferred_element_type=f32))
            c = c_scr[n]
            c_wide = jnp.concatenate([c] * (t // LANES), axis=1)
            d = (z - suffix) - c_wide
            if diagonal:
                d = jnp.where(causal, d, MASKED)
            d_scr[n] = d
            c_scr[n] = c + jnp.sum(sp, axis=-1, keepdims=True)

    def weigh(m):
        for n, (b, _) in enumerate(streams):
            a = jnp.exp(d_scr[n]).astype(bf16)
            acc_scr[n] += jnp.dot(a, v_ref[b, rows_of_tile(m), :], preferred_element_type=f32)

    c_scr[...] = jnp.zeros(c_scr.shape, f32)
    acc_scr[...] = jnp.zeros(acc_scr.shape, f32)
    scores(0)
    decay(True)
    scores(1)

    @pl.loop(0, i)
    def _(m):
        weigh(m)
        decay(False)
        scores(m + 2)

    weigh(i)

    for b in range(q_ref.shape[0]):
        out = jnp.where(lo_half, acc_scr[2 * b], acc_scr[2 * b + 1])
        y = _pair_group_norm(out, lo_half) * g_ref[...]
        a_z = az_ref[b]
        o_ref[b] = (y * (a_z * _sigmoid(a_z))).astype(bf16)


def _attention(q, k, v, a_z, bg_attn):
    bsz, s, d_sb = q.shape
    pairs = d_sb // LANES
    n_streams = ATTN_BATCH * (LANES // HEAD_DIM)
    tile_spec = pl.BlockSpec((ATTN_BATCH, ATTN_TILE, LANES), lambda b, p, i: (b, i, p))
    seq_spec = pl.BlockSpec((ATTN_BATCH, s, LANES), lambda b, p, i: (b, 0, p))
    return pl.pallas_call(
        _attn_kernel,
        grid=(bsz // ATTN_BATCH, pairs, s // ATTN_TILE),
        in_specs=[tile_spec, seq_spec, seq_spec, tile_spec,
                  pl.BlockSpec((1, LANES), lambda b, p, i: (0, p))],
        out_specs=tile_spec,
        out_shape=jax.ShapeDtypeStruct((bsz, s, d_sb), bf16),
        scratch_shapes=[pltpu.VMEM((n_streams, ATTN_TILE, ATTN_TILE), f32),
                        pltpu.VMEM((n_streams, ATTN_TILE, ATTN_TILE), f32),
                        pltpu.VMEM((n_streams, ATTN_TILE, LANES), f32),
                        pltpu.VMEM((n_streams, ATTN_TILE, LANES), f32)],
        compiler_params=pltpu.CompilerParams(
            dimension_semantics=("arbitrary", "arbitrary", "arbitrary"),
            vmem_limit_bytes=VMEM_LIMIT),
        name="attention",
    )(q, k, v, a_z, bg_attn)


def _out_proj_kernel(x_ref, yc_ref, ya_ref, p_ref, wo_ref, pg_ref, wpg_ref, bpg_ref, wpe_ref,
                     fg_ref, o_ref, *, d_conv, final):
    x1 = (x_ref[...]
          + jnp.dot(yc_ref[...], wo_ref[0:d_conv, :], preferred_element_type=f32)
          + jnp.dot(ya_ref[...], wo_ref[d_conv:, :], preferred_element_type=f32))
    h = _rmsnorm(x1, pg_ref[...]).astype(bf16)
    gate = _sigmoid(jnp.dot(h, wpg_ref[...], preferred_element_type=f32) + bpg_ref[...])
    pe = jnp.dot(p_ref[...].astype(bf16), wpe_ref[...], preferred_element_type=f32)
    x2 = x1 + gate * pe
    o_ref[...] = _rmsnorm(x2, fg_ref[...]) if final else x2


def _out_proj(x, y_c, y_a, p, w_out, ple_g, w_pg, b_pg, w_pe, final_g, final):
    n, d = x.shape
    d_conv, d_sb, ple = y_c.shape[1], y_a.shape[1], p.shape[1]
    row_spec = lambda width: pl.BlockSpec((ROW_TILE, width), lambda t: (t, 0))
    const = lambda shape: pl.BlockSpec(shape, lambda t: (0, 0), pipeline_mode=pl.Buffered(1))
    return pl.pallas_call(
        functools.partial(_out_proj_kernel, d_conv=d_conv, final=final),
        grid=(n // ROW_TILE,),
        in_specs=[row_spec(d), row_spec(d_conv), row_spec(d_sb), row_spec(ple),
                  const((d_conv + d_sb, d)), const((1, d)), const((d, d)), const((1, d)),
                  const((ple, d)), const((1, d))],
        out_specs=row_spec(d),
        out_shape=jax.ShapeDtypeStruct((n, d), f32),
        compiler_params=pltpu.CompilerParams(
            dimension_semantics=("arbitrary",), vmem_limit_bytes=VMEM_LIMIT),
        name="out_proj",
    )(x, y_c, y_a, p, w_out, ple_g, w_pg, b_pg, w_pe, final_g)


def kernel(x, p, norm_g, w_in, conv_w, conv_b, branch_g, w_out, ple_norm_g, w_pg, b_pg, w_pe, final_g):
    depth = w_in.shape[0]
    bsz, s, d = x.shape
    d_conv = conv_w.shape[2]
    d_sb = branch_g.shape[1] - d_conv
    assert s % ROW_TILE == 0 and s % ATTN_TILE == 0 and d_sb % LANES == 0 and d_conv % LANES == 0
    row = lambda vec: vec.reshape(1, -1)
    for i in range(depth):
        y_c, q, k, v, a_z = _in_proj(
            x, row(norm_g[i]), w_in[i].astype(bf16), conv_w[i], row(conv_b[i]),
            row(branch_g[i, :d_conv]), d_conv, d_sb)
        y_a = _attention(q, k, v, a_z, row(branch_g[i, d_conv:]))
        x = _out_proj(
            x.reshape(bsz * s, d), y_c.reshape(bsz * s, d_conv), y_a.reshape(bsz * s, d_sb),
            p[i].reshape(bsz * s, -1), w_out[i].astype(bf16), row(ple_norm_g[i]),
            w_pg[i].astype(bf16), row(b_pg[i]), w_pe[i].astype(bf16), row(final_g),
            final=(i == depth - 1)).reshape(bsz, s, d)
    return x
```

```python
import functools

import jax
import jax.numpy as jnp
from jax import lax
from jax.experimental import pallas as pl
from jax.experimental.pallas import tpu as pltpu

HEAD_DIM = 64
LANES = 128
CONV_WIDTH = 3
EPS = 1e-6

ROW_TILE = 512
ATTN_TILE = 256
ATTN_BATCH = 4
CARRY_ROWS = 8
VMEM_LIMIT = 48 * 1024 * 1024
LOG2_E = 1.4426950408889634

f32 = jnp.float32
bf16 = jnp.bfloat16


def _rmsnorm(x, g):
    return x * lax.rsqrt(jnp.mean(x * x, axis=-1, keepdims=True) + EPS) * g


def _sigmoid(x):
    return 1.0 / (1.0 + jnp.exp(-x))


def _pair_group_norm(y, lo_half):
    sq = y * y
    s_lo = jnp.sum(jnp.where(lo_half, sq, 0.0), axis=-1, keepdims=True)
    s_hi = jnp.sum(jnp.where(lo_half, 0.0, sq), axis=-1, keepdims=True)
    inv_lo = lax.rsqrt(s_lo * (1.0 / HEAD_DIM) + EPS)
    inv_hi = lax.rsqrt(s_hi * (1.0 / HEAD_DIM) + EPS)
    return y * jnp.where(lo_half, inv_lo, inv_hi)


def _in_proj_kernel(x_ref, g_ref, w_ref, cw_ref, cb_ref, bg_ref,
                    yc_ref, q_ref, k_ref, v_ref, az_ref, u_scr, *, d_conv, d_sb):
    rows = x_ref.shape[1]
    h = _rmsnorm(x_ref[0], g_ref[...]).astype(bf16)

    def proj(col0, width):
        return jnp.dot(h, w_ref[:, col0:col0 + width], preferred_element_type=f32)

    c_b = proj(0, d_conv)
    u = proj(d_conv, d_conv) * proj(2 * d_conv, d_conv)
    c_z = proj(3 * d_conv, d_conv)
    base = 4 * d_conv
    q_ref[0] = (proj(base, d_sb) * (HEAD_DIM ** -0.5 * LOG2_E)).astype(bf16)
    k_ref[0] = proj(base + d_sb, d_sb).astype(bf16)
    v_ref[0] = proj(base + 2 * d_sb, d_sb).astype(bf16)
    az_ref[0] = proj(base + 3 * d_sb, d_sb)

    @pl.when(pl.program_id(1) == 0)
    def _():
        u_scr[0:CARRY_ROWS, :] = jnp.zeros((CARRY_ROWS, d_conv), f32)

    u_scr[CARRY_ROWS:CARRY_ROWS + rows, :] = u
    conv = cb_ref[...]
    for j in range(CONV_WIDTH):
        back = CONV_WIDTH - 1 - j
        conv = conv + u_scr[CARRY_ROWS - back:CARRY_ROWS - back + rows, :] * cw_ref[j:j + 1, :]
    u_scr[0:CARRY_ROWS, :] = u_scr[rows:rows + CARRY_ROWS, :]

    y_c = c_b * conv
    gate = c_z * _sigmoid(c_z)
    lo_half = lax.broadcasted_iota(jnp.int32, (1, LANES), 1) < HEAD_DIM
    for j in range(d_conv // LANES):
        sl = slice(j * LANES, (j + 1) * LANES)
        y = _pair_group_norm(y_c[:, sl], lo_half) * bg_ref[:, sl]
        yc_ref[0, :, sl] = (y * gate[:, sl]).astype(bf16)


def _in_proj(x, norm_g, w_in, conv_w, conv_b, bg_conv, d_conv, d_sb):
    bsz, s, d = x.shape
    n_in = w_in.shape[1]
    row_spec = lambda width: pl.BlockSpec((1, ROW_TILE, width), lambda b, t: (b, t, 0))
    const = lambda shape: pl.BlockSpec(shape, lambda b, t: (0, 0), pipeline_mode=pl.Buffered(1))
    return pl.pallas_call(
        functools.partial(_in_proj_kernel, d_conv=d_conv, d_sb=d_sb),
        grid=(bsz, s // ROW_TILE),
        in_specs=[row_spec(d), const((1, d)), const((d, n_in)),
                  const((CONV_WIDTH, d_conv)), const((1, d_conv)), const((1, d_conv))],
        out_specs=[row_spec(d_conv), row_spec(d_sb), row_spec(d_sb), row_spec(d_sb), row_spec(d_sb)],
        out_shape=[jax.ShapeDtypeStruct((bsz, s, d_conv), bf16),
                   jax.ShapeDtypeStruct((bsz, s, d_sb), bf16),
                   jax.ShapeDtypeStruct((bsz, s, d_sb), bf16),
                   jax.ShapeDtypeStruct((bsz, s, d_sb), bf16),
                   jax.ShapeDtypeStruct((bsz, s, d_sb), f32)],
        scratch_shapes=[pltpu.VMEM((ROW_TILE + CARRY_ROWS, d_conv), f32)],
        compiler_params=pltpu.CompilerParams(
            dimension_semantics=("arbitrary", "arbitrary"), vmem_limit_bytes=VMEM_LIMIT),
        name="in_proj",
    )(x, norm_g, w_in, conv_w, conv_b, bg_conv)


MASKED = -1e30


def _attn_kernel(q_ref, k_ref, v_ref, az_ref, g_ref, o_ref, z_scr, d_scr, c_scr, acc_scr):
    t = ATTN_TILE
    i = pl.program_id(2)
    lo_half = lax.broadcasted_iota(jnp.int32, (1, LANES), 1) < HEAD_DIM
    key_pos = lax.broadcasted_iota(jnp.int32, (t, t), 1)
    query_pos = lax.broadcasted_iota(jnp.int32, (t, t), 0)
    causal = key_pos < query_pos
    keys_from = (query_pos >= key_pos).astype(bf16)
    streams = []
    for b in range(q_ref.shape[0]):
        q = q_ref[b]
        zero = jnp.zeros_like(q)
        streams.append((b, jnp.where(lo_half, q, zero)))
        streams.append((b, jnp.where(lo_half, zero, q)))

    def rows_of_tile(m):
        return pl.ds(pl.multiple_of(jnp.maximum(i - m, 0) * t, t), t)

    def scores(n, m):
        b, q_h = streams[n]
        z_scr[n] = lax.dot_general(q_h, k_ref[b, rows_of_tile(m), :], (((1,), (1,)), ((), ())),
                                   preferred_element_type=f32)

    def decay(n, diagonal):
        z = z_scr[n]
        c = c_scr[n]
        d_scr[n] = z - jnp.concatenate([c] * (t // LANES), axis=1)
        neg_abs = lax.bitcast_convert_type(
            lax.bitcast_convert_type(z, jnp.int32) | jnp.int32(-2 ** 31), f32)
        sp = jnp.maximum(z, 0.0) + jnp.log(1.0 + jnp.exp2(neg_abs)) * LOG2_E
        if diagonal:
            sp = jnp.where(causal, sp, 0.0)
        suffix = jnp.dot(sp.astype(bf16), keys_from, preferred_element_type=f32)
        d = d_scr[n] - suffix
        if diagonal:
            d = jnp.where(causal, d, MASKED)
        d_scr[n] = d
        c_scr[n] = c + suffix[:, 0:1]

    def weigh(n, m):
        a = jnp.exp2(d_scr[n]).astype(bf16)
        acc_scr[n] += jnp.dot(a, v_ref[streams[n][0], rows_of_tile(m), :],
                              preferred_element_type=f32)

    c_scr[...] = jnp.zeros(c_scr.shape, f32)
    acc_scr[...] = jnp.zeros(acc_scr.shape, f32)
    for n in range(len(streams)):
        scores(n, 0)
    for n in range(len(streams)):
        decay(n, True)
        scores(n, 1)

    @pl.loop(0, i)
    def _(m):
        for n in range(len(streams)):
            weigh(n, m)
            decay(n, False)
            scores(n, m + 2)

    for n in range(len(streams)):
        weigh(n, i)

    for b in range(q_ref.shape[0]):
        out = jnp.where(lo_half, acc_scr[2 * b], acc_scr[2 * b + 1])
        y = _pair_group_norm(out, lo_half) * g_ref[...]
        a_z = az_ref[b]
        o_ref[b] = (y * (a_z * _sigmoid(a_z))).astype(bf16)


def _attention(q, k, v, a_z, bg_attn):
    bsz, s, d_sb = q.shape
    pairs = d_sb // LANES
    n_streams = ATTN_BATCH * (LANES // HEAD_DIM)
    tile_spec = pl.BlockSpec((ATTN_BATCH, ATTN_TILE, LANES), lambda b, p, i: (b, i, p))
    seq_spec = pl.BlockSpec((ATTN_BATCH, s, LANES), lambda b, p, i: (b, 0, p))
    return pl.pallas_call(
        _attn_kernel,
        grid=(bsz // ATTN_BATCH, pairs, s // ATTN_TILE),
        in_specs=[tile_spec, seq_spec, seq_spec, tile_spec,
                  pl.BlockSpec((1, LANES), lambda b, p, i: (0, p))],
        out_specs=tile_spec,
        out_shape=jax.ShapeDtypeStruct((bsz, s, d_sb), bf16),
        scratch_shapes=[pltpu.VMEM((n_streams, ATTN_TILE, ATTN_TILE), f32),
                        pltpu.VMEM((n_streams, ATTN_TILE, ATTN_TILE), f32),
                        pltpu.VMEM((n_streams, ATTN_TILE, LANES), f32),
                        pltpu.VMEM((n_streams, ATTN_TILE, LANES), f32)],
        compiler_params=pltpu.CompilerParams(
            dimension_semantics=("arbitrary", "arbitrary", "arbitrary"),
            vmem_limit_bytes=VMEM_LIMIT),
        name="attention",
    )(q, k, v, a_z, bg_attn)


def _out_proj_kernel(x_ref, yc_ref, ya_ref, p_ref, wo_ref, pg_ref, wpg_ref, bpg_ref, wpe_ref,
                     fg_ref, o_ref, *, d_conv, final):
    x1 = (x_ref[...]
          + jnp.dot(yc_ref[...], wo_ref[0:d_conv, :], preferred_element_type=f32)
          + jnp.dot(ya_ref[...], wo_ref[d_conv:, :], preferred_element_type=f32))
    h = _rmsnorm(x1, pg_ref[...]).astype(bf16)
    gate = _sigmoid(jnp.dot(h, wpg_ref[...], preferred_element_type=f32) + bpg_ref[...])
    pe = jnp.dot(p_ref[...].astype(bf16), wpe_ref[...], preferred_element_type=f32)
    x2 = x1 + gate * pe
    o_ref[...] = _rmsnorm(x2, fg_ref[...]) if final else x2


def _out_proj(x, y_c, y_a, p, w_out, ple_g, w_pg, b_pg, w_pe, final_g, final):
    n, d = x.shape
    d_conv, d_sb, ple = y_c.shape[1], y_a.shape[1], p.shape[1]
    row_spec = lambda width: pl.BlockSpec((ROW_TILE, width), lambda t: (t, 0))
    const = lambda shape: pl.BlockSpec(shape, lambda t: (0, 0), pipeline_mode=pl.Buffered(1))
    return pl.pallas_call(
        functools.partial(_out_proj_kernel, d_conv=d_conv, final=final),
        grid=(n // ROW_TILE,),
        in_specs=[row_spec(d), row_spec(d_conv), row_spec(d_sb), row_spec(ple),
                  const((d_conv + d_sb, d)), const((1, d)), const((d, d)), const((1, d)),
                  const((ple, d)), const((1, d))],
        out_specs=row_spec(d),
        out_shape=jax.ShapeDtypeStruct((n, d), f32),
        compiler_params=pltpu.CompilerParams(
            dimension_semantics=("arbitrary",), vmem_limit_bytes=VMEM_LIMIT),
        name="out_proj",
    )(x, y_c, y_a, p, w_out, ple_g, w_pg, b_pg, w_pe, final_g)


def kernel(x, p, norm_g, w_in, conv_w, conv_b, branch_g, w_out, ple_norm_g, w_pg, b_pg, w_pe, final_g):
    depth = w_in.shape[0]
    bsz, s, d = x.shape
    d_conv = conv_w.shape[2]
    d_sb = branch_g.shape[1] - d_conv
    assert s % ROW_TILE == 0 and s % ATTN_TILE == 0 and d_sb % LANES == 0 and d_conv % LANES == 0
    row = lambda vec: vec.reshape(1, -1)
    for i in range(depth):
        y_c, q, k, v, a_z = _in_proj(
            x, row(norm_g[i]), w_in[i].astype(bf16), conv_w[i], row(conv_b[i]),
            row(branch_g[i, :d_conv]), d_conv, d_sb)
        y_a = _attention(q, k, v, a_z, row(branch_g[i, d_conv:]))
        x = _out_proj(
            x.reshape(bsz * s, d), y_c.reshape(bsz * s, d_conv), y_a.reshape(bsz * s, d_sb),
            p[i].reshape(bsz * s, -1), w_out[i].astype(bf16), row(ple_norm_g[i]),
            w_pg[i].astype(bf16), row(b_pg[i]), w_pe[i].astype(bf16), row(final_g),
            final=(i == depth - 1)).reshape(bsz, s, d)
    return x
```

```python
import functools

import jax
import jax.numpy as jnp
from jax import lax
from jax.experimental import pallas as pl
from jax.experimental.pallas import tpu as pltpu

HEAD_DIM = 64
LANES = 128
CONV_WIDTH = 3
EPS = 1e-6

ROW_TILE = 512
ATTN_TILE = 256
LOOP_STEPS = 2
ATTN_BATCH = 4
CARRY_ROWS = 8
VMEM_LIMIT = 48 * 1024 * 1024
LOG2_E = 1.4426950408889634

f32 = jnp.float32
bf16 = jnp.bfloat16


def _rmsnorm(x, g):
    return x * lax.rsqrt(jnp.mean(x * x, axis=-1, keepdims=True) + EPS) * g


def _sigmoid(x):
    return 1.0 / (1.0 + jnp.exp(-x))


def _pair_group_norm(y, lo_half):
    sq = y * y
    s_lo = jnp.sum(jnp.where(lo_half, sq, 0.0), axis=-1, keepdims=True)
    s_hi = jnp.sum(jnp.where(lo_half, 0.0, sq), axis=-1, keepdims=True)
    inv_lo = lax.rsqrt(s_lo * (1.0 / HEAD_DIM) + EPS)
    inv_hi = lax.rsqrt(s_hi * (1.0 / HEAD_DIM) + EPS)
    return y * jnp.where(lo_half, inv_lo, inv_hi)


def _in_proj_kernel(x_ref, g_ref, w_ref, cw_ref, cb_ref, bg_ref,
                    yc_ref, q_ref, k_ref, v_ref, az_ref, u_scr, *, d_conv, d_sb):
    rows = x_ref.shape[1]

    @pl.when(pl.program_id(1) == 0)
    def _():
        u_scr[0:CARRY_ROWS, :] = jnp.zeros((CARRY_ROWS, d_conv), f32)

    h = _rmsnorm(x_ref[0], g_ref[...]).astype(bf16)

    def proj(col0, width):
        return jnp.dot(h, w_ref[:, col0:col0 + width], preferred_element_type=f32)

    c_b = proj(0, d_conv)
    u = proj(d_conv, d_conv) * proj(2 * d_conv, d_conv)
    c_z = proj(3 * d_conv, d_conv)
    u_scr[CARRY_ROWS:CARRY_ROWS + rows, :] = u
    lo_half = lax.broadcasted_iota(jnp.int32, (1, LANES), 1) < HEAD_DIM

    def conv_block(j):
        sl = slice(j * LANES, (j + 1) * LANES)
        conv = cb_ref[:, sl]
        for tap in range(CONV_WIDTH):
            back = CONV_WIDTH - 1 - tap
            conv = conv + u_scr[CARRY_ROWS - back:CARRY_ROWS - back + rows, sl] * cw_ref[tap:tap + 1, sl]
        y = _pair_group_norm(c_b[:, sl] * conv, lo_half) * bg_ref[:, sl]
        gate = c_z[:, sl]
        yc_ref[0, :, sl] = (y * (gate * _sigmoid(gate))).astype(bf16)

    attn_outs = ((q_ref, HEAD_DIM ** -0.5), (k_ref, 1.0), (v_ref, 1.0), (az_ref, 1.0))

    def attn_block(j):
        ref, scale = attn_outs[j]
        ref[0] = (proj(4 * d_conv + j * d_sb, d_sb) * scale).astype(ref.dtype)

    n_conv_blocks = d_conv // LANES
    for j in range(max(n_conv_blocks, len(attn_outs))):
        if j < n_conv_blocks:
            conv_block(j)
        if j < len(attn_outs):
            attn_block(j)
    u_scr[0:CARRY_ROWS, :] = u_scr[rows:rows + CARRY_ROWS, :]


def _in_proj(x, norm_g, w_in, conv_w, conv_b, bg_conv, d_conv, d_sb):
    bsz, s, d = x.shape
    n_in = w_in.shape[1]
    row_spec = lambda width: pl.BlockSpec((1, ROW_TILE, width), lambda b, t: (b, t, 0))
    const = lambda shape: pl.BlockSpec(shape, lambda b, t: (0, 0), pipeline_mode=pl.Buffered(1))
    return pl.pallas_call(
        functools.partial(_in_proj_kernel, d_conv=d_conv, d_sb=d_sb),
        grid=(bsz, s // ROW_TILE),
        in_specs=[row_spec(d), const((1, d)), const((d, n_in)),
                  const((CONV_WIDTH, d_conv)), const((1, d_conv)), const((1, d_conv))],
        out_specs=[row_spec(d_conv), row_spec(d_sb), row_spec(d_sb), row_spec(d_sb), row_spec(d_sb)],
        out_shape=[jax.ShapeDtypeStruct((bsz, s, d_conv), bf16),
                   jax.ShapeDtypeStruct((bsz, s, d_sb), bf16),
                   jax.ShapeDtypeStruct((bsz, s, d_sb), bf16),
                   jax.ShapeDtypeStruct((bsz, s, d_sb), bf16),
                   jax.ShapeDtypeStruct((bsz, s, d_sb), f32)],
        scratch_shapes=[pltpu.VMEM((ROW_TILE + CARRY_ROWS, d_conv), f32)],
        compiler_params=pltpu.CompilerParams(
            dimension_semantics=("arbitrary", "arbitrary"), vmem_limit_bytes=VMEM_LIMIT),
        name="in_proj",
    )(x, norm_g, w_in, conv_w, conv_b, bg_conv)


MASKED = -1e30


def _attn_kernel(q_ref, k_ref, v_ref, az_ref, g_ref, o_ref, z_scr, d_scr, c_scr, acc_scr):
    t = ATTN_TILE
    i = pl.program_id(2)
    lo_half = lax.broadcasted_iota(jnp.int32, (1, LANES), 1) < HEAD_DIM
    key_pos = lax.broadcasted_iota(jnp.int32, (t, t), 1)
    query_pos = lax.broadcasted_iota(jnp.int32, (t, t), 0)
    causal = key_pos < query_pos
    keys_from = (query_pos >= key_pos).astype(bf16)
    streams = []
    for b in range(q_ref.shape[0]):
        q = q_ref[b]
        zero = jnp.zeros_like(q)
        streams.append((b, jnp.where(lo_half, q, zero)))
        streams.append((b, jnp.where(lo_half, zero, q)))

    def rows_of_tile(m):
        return pl.ds(pl.multiple_of(jnp.maximum(i - m, 0) * t, t), t)

    def scores(n, m):
        b, q_h = streams[n]
        z_scr[n] = lax.dot_general(q_h, k_ref[b, rows_of_tile(m), :], (((1,), (1,)), ((), ())),
                                   preferred_element_type=f32)

    def decay(n, diagonal):
        z = z_scr[n]
        c = c_scr[n]
        d_scr[n] = z - jnp.concatenate([c] * (t // LANES), axis=1)
        sp = jnp.maximum(z, 0.0) + jnp.log(1.0 + jnp.exp2(jnp.abs(z) * -LOG2_E))
        if diagonal:
            sp = jnp.where(causal, sp, 0.0)
        suffix = jnp.dot(sp.astype(bf16), keys_from, preferred_element_type=f32)
        d = d_scr[n] - suffix
        if diagonal:
            d = jnp.where(causal, d, MASKED)
        d_scr[n] = d
        c_scr[n] = c + suffix[:, 0:1]

    def weigh(n, m):
        a = jnp.exp(d_scr[n]).astype(bf16)
        acc_scr[n] += jnp.dot(a, v_ref[streams[n][0], rows_of_tile(m), :],
                              preferred_element_type=f32)

    c_scr[...] = jnp.zeros(c_scr.shape, f32)
    acc_scr[...] = jnp.zeros(acc_scr.shape, f32)
    for n in range(len(streams)):
        scores(n, 0)
    for n in range(len(streams)):
        decay(n, True)
        scores(n, 1)

    def step(m):
        for n in range(len(streams)):
            weigh(n, m)
            decay(n, False)
            scores(n, m + 2)

    peeled = i % LOOP_STEPS
    for r in range(LOOP_STEPS - 1):
        pl.when(r < peeled)(functools.partial(step, r))

    @pl.loop(0, i // LOOP_STEPS)
    def _(trip):
        for u in range(LOOP_STEPS):
            step(peeled + LOOP_STEPS * trip + u)

    for n in range(len(streams)):
        weigh(n, i)

    for b in range(q_ref.shape[0]):
        out = jnp.where(lo_half, acc_scr[2 * b], acc_scr[2 * b + 1])
        y = _pair_group_norm(out, lo_half) * g_ref[...]
        a_z = az_ref[b]
        o_ref[b] = (y * (a_z * _sigmoid(a_z))).astype(bf16)


def _attention(q, k, v, a_z, bg_attn):
    bsz, s, d_sb = q.shape
    pairs = d_sb // LANES
    n_streams = ATTN_BATCH * (LANES // HEAD_DIM)
    tile_spec = pl.BlockSpec((ATTN_BATCH, ATTN_TILE, LANES), lambda b, p, i: (b, i, p))
    seq_spec = pl.BlockSpec((ATTN_BATCH, s, LANES), lambda b, p, i: (b, 0, p))
    return pl.pallas_call(
        _attn_kernel,
        grid=(bsz // ATTN_BATCH, pairs, s // ATTN_TILE),
        in_specs=[tile_spec, seq_spec, seq_spec, tile_spec,
                  pl.BlockSpec((1, LANES), lambda b, p, i: (0, p))],
        out_specs=tile_spec,
        out_shape=jax.ShapeDtypeStruct((bsz, s, d_sb), bf16),
        scratch_shapes=[pltpu.VMEM((n_streams, ATTN_TILE, ATTN_TILE), f32),
                        pltpu.VMEM((n_streams, ATTN_TILE, ATTN_TILE), f32),
                        pltpu.VMEM((n_streams, ATTN_TILE, LANES), f32),
                        pltpu.VMEM((n_streams, ATTN_TILE, LANES), f32)],
        compiler_params=pltpu.CompilerParams(
            dimension_semantics=("arbitrary", "arbitrary", "arbitrary"),
            vmem_limit_bytes=VMEM_LIMIT),
        name="attention",
    )(q, k, v, a_z, bg_attn)


def _out_proj_kernel(x_ref, yc_ref, ya_ref, p_ref, wo_ref, pg_ref, wpg_ref, bpg_ref, wpe_ref,
                     fg_ref, o_ref, *, d_conv, final):
    halves = [pl.ds(r * (x_ref.shape[0] // 2), x_ref.shape[0] // 2) for r in range(2)]
    x1 = [x_ref[rows, :]
          + jnp.dot(yc_ref[rows, :], wo_ref[0:d_conv, :], preferred_element_type=f32)
          + jnp.dot(ya_ref[rows, :], wo_ref[d_conv:, :], preferred_element_type=f32)
          for rows in halves]
    for rows, x1_r in zip(halves, x1):
        h = _rmsnorm(x1_r, pg_ref[...]).astype(bf16)
        gate = _sigmoid(jnp.dot(h, wpg_ref[...], preferred_element_type=f32) + bpg_ref[...])
        pe = jnp.dot(p_ref[rows, :].astype(bf16), wpe_ref[...], preferred_element_type=f32)
        x2 = x1_r + gate * pe
        o_ref[rows, :] = _rmsnorm(x2, fg_ref[...]) if final else x2


def _out_proj(x, y_c, y_a, p, w_out, ple_g, w_pg, b_pg, w_pe, final_g, final):
    n, d = x.shape
    d_conv, d_sb, ple = y_c.shape[1], y_a.shape[1], p.shape[1]
    row_spec = lambda width: pl.BlockSpec((ROW_TILE, width), lambda t: (t, 0))
    const = lambda shape: pl.BlockSpec(shape, lambda t: (0, 0), pipeline_mode=pl.Buffered(1))
    return pl.pallas_call(
        functools.partial(_out_proj_kernel, d_conv=d_conv, final=final),
        grid=(n // ROW_TILE,),
        in_specs=[row_spec(d), row_spec(d_conv), row_spec(d_sb), row_spec(ple),
                  const((d_conv + d_sb, d)), const((1, d)), const((d, d)), const((1, d)),
                  const((ple, d)), const((1, d))],
        out_specs=row_spec(d),
        out_shape=jax.ShapeDtypeStruct((n, d), f32),
        compiler_params=pltpu.CompilerParams(
            dimension_semantics=("arbitrary",), vmem_limit_bytes=VMEM_LIMIT),
        name="out_proj",
    )(x, y_c, y_a, p, w_out, ple_g, w_pg, b_pg, w_pe, final_g)


def kernel(x, p, norm_g, w_in, conv_w, conv_b, branch_g, w_out, ple_norm_g, w_pg, b_pg, w_pe, final_g):
    depth = w_in.shape[0]
    bsz, s, d = x.shape
    d_conv = conv_w.shape[2]
    d_sb = branch_g.shape[1] - d_conv
    assert s % ROW_TILE == 0 and s % ATTN_TILE == 0 and d_sb % LANES == 0 and d_conv % LANES == 0
    row = lambda vec: vec.reshape(1, -1)
    for i in range(depth):
        y_c, q, k, v, a_z = _in_proj(
            x, row(norm_g[i]), w_in[i].astype(bf16), conv_w[i], row(conv_b[i]),
            row(branch_g[i, :d_conv]), d_conv, d_sb)
        y_a = _attention(q, k, v, a_z, row(branch_g[i, d_conv:]))
        x = _out_proj(
            x.reshape(bsz * s, d), y_c.reshape(bsz * s, d_conv), y_a.reshape(bsz * s, d_sb),
            p[i].reshape(bsz * s, -1), w_out[i].astype(bf16), row(ple_norm_g[i]),
            w_pg[i].astype(bf16), row(b_pg[i]), w_pe[i].astype(bf16), row(final_g),
            final=(i == depth - 1)).reshape(bsz, s, d)
    return x
```

```python
import functools

import jax
import jax.numpy as jnp
from jax import lax
from jax.experimental import pallas as pl
from jax.experimental.pallas import tpu as pltpu

HEAD_DIM = 64
LANES = 128
CONV_WIDTH = 3
EPS = 1e-6

ROW_TILE = 512
ATTN_TILE = 256
DEAD_CARRY = 128.0
ATTN_BATCH = 4
CARRY_ROWS = 8
VMEM_LIMIT = 48 * 1024 * 1024
LOG2_E = 1.4426950408889634

f32 = jnp.float32
bf16 = jnp.bfloat16


def _rmsnorm(x, g):
    return x * lax.rsqrt(jnp.mean(x * x, axis=-1, keepdims=True) + EPS) * g


def _sigmoid(x):
    return 1.0 / (1.0 + jnp.exp(-x))


def _pair_group_norm(y, lo_half):
    sq = y * y
    s_lo = jnp.sum(jnp.where(lo_half, sq, 0.0), axis=-1, keepdims=True)
    s_hi = jnp.sum(jnp.where(lo_half, 0.0, sq), axis=-1, keepdims=True)
    inv_lo = lax.rsqrt(s_lo * (1.0 / HEAD_DIM) + EPS)
    inv_hi = lax.rsqrt(s_hi * (1.0 / HEAD_DIM) + EPS)
    return y * jnp.where(lo_half, inv_lo, inv_hi)


def _in_proj_kernel(x_ref, g_ref, w_ref, cw_ref, cb_ref, bg_ref,
                    yc_ref, q_ref, k_ref, v_ref, az_ref, u_scr, *, d_conv, d_sb):
    rows = x_ref.shape[1]

    @pl.when(pl.program_id(1) == 0)
    def _():
        u_scr[0:CARRY_ROWS, :] = jnp.zeros((CARRY_ROWS, d_conv), f32)

    h = _rmsnorm(x_ref[0], g_ref[...]).astype(bf16)

    def proj(col0, width):
        return jnp.dot(h, w_ref[:, col0:col0 + width], preferred_element_type=f32)

    c_b = proj(0, d_conv)
    u = proj(d_conv, d_conv) * proj(2 * d_conv, d_conv)
    c_z = proj(3 * d_conv, d_conv)
    u_scr[CARRY_ROWS:CARRY_ROWS + rows, :] = u
    lo_half = lax.broadcasted_iota(jnp.int32, (1, LANES), 1) < HEAD_DIM

    def conv_block(j):
        sl = slice(j * LANES, (j + 1) * LANES)
        conv = cb_ref[:, sl]
        for tap in range(CONV_WIDTH):
            back = CONV_WIDTH - 1 - tap
            conv = conv + u_scr[CARRY_ROWS - back:CARRY_ROWS - back + rows, sl] * cw_ref[tap:tap + 1, sl]
        y = _pair_group_norm(c_b[:, sl] * conv, lo_half) * bg_ref[:, sl]
        gate = c_z[:, sl]
        yc_ref[0, :, sl] = (y * (gate * _sigmoid(gate))).astype(bf16)

    attn_outs = ((q_ref, HEAD_DIM ** -0.5), (k_ref, 1.0), (v_ref, 1.0), (az_ref, 1.0))

    def attn_block(j):
        ref, scale = attn_outs[j]
        ref[0] = (proj(4 * d_conv + j * d_sb, d_sb) * scale).astype(ref.dtype)

    n_conv_blocks = d_conv // LANES
    for j in range(max(n_conv_blocks, len(attn_outs))):
        if j < n_conv_blocks:
            conv_block(j)
        if j < len(attn_outs):
            attn_block(j)
    u_scr[0:CARRY_ROWS, :] = u_scr[rows:rows + CARRY_ROWS, :]


def _in_proj(x, norm_g, w_in, conv_w, conv_b, bg_conv, d_conv, d_sb):
    bsz, s, d = x.shape
    n_in = w_in.shape[1]
    row_spec = lambda width: pl.BlockSpec((1, ROW_TILE, width), lambda b, t: (b, t, 0))
    const = lambda shape: pl.BlockSpec(shape, lambda b, t: (0, 0), pipeline_mode=pl.Buffered(1))
    return pl.pallas_call(
        functools.partial(_in_proj_kernel, d_conv=d_conv, d_sb=d_sb),
        grid=(bsz, s // ROW_TILE),
        in_specs=[row_spec(d), const((1, d)), const((d, n_in)),
                  const((CONV_WIDTH, d_conv)), const((1, d_conv)), const((1, d_conv))],
        out_specs=[row_spec(d_conv), row_spec(d_sb), row_spec(d_sb), row_spec(d_sb), row_spec(d_sb)],
        out_shape=[jax.ShapeDtypeStruct((bsz, s, d_conv), bf16),
                   jax.ShapeDtypeStruct((bsz, s, d_sb), bf16),
                   jax.ShapeDtypeStruct((bsz, s, d_sb), bf16),
                   jax.ShapeDtypeStruct((bsz, s, d_sb), bf16),
                   jax.ShapeDtypeStruct((bsz, s, d_sb), f32)],
        scratch_shapes=[pltpu.VMEM((ROW_TILE + CARRY_ROWS, d_conv), f32)],
        compiler_params=pltpu.CompilerParams(
            dimension_semantics=("arbitrary", "arbitrary"), vmem_limit_bytes=VMEM_LIMIT),
        name="in_proj",
    )(x, norm_g, w_in, conv_w, conv_b, bg_conv)


MASKED = -1e30


def _attn_kernel(q_ref, k_ref, v_ref, az_ref, g_ref, o_ref, z_scr, d_scr, c_scr, acc_scr):
    t = ATTN_TILE
    i = pl.program_id(2)
    lo_half = lax.broadcasted_iota(jnp.int32, (1, LANES), 1) < HEAD_DIM
    key_pos = lax.broadcasted_iota(jnp.int32, (t, t), 1)
    query_pos = lax.broadcasted_iota(jnp.int32, (t, t), 0)
    causal = key_pos < query_pos
    keys_from = (query_pos >= key_pos).astype(bf16)
    streams = []
    for b in range(q_ref.shape[0]):
        q = q_ref[b]
        zero = jnp.zeros_like(q)
        streams.append((b, jnp.where(lo_half, q, zero)))
        streams.append((b, jnp.where(lo_half, zero, q)))

    def rows_of_tile(m):
        return pl.ds(pl.multiple_of(jnp.maximum(i - m, 0) * t, t), t)

    def scores(n, m):
        b, q_h = streams[n]
        z_scr[n] = lax.dot_general(q_h, k_ref[b, rows_of_tile(m), :], (((1,), (1,)), ((), ())),
                                   preferred_element_type=f32)

    def decay(n, diagonal):
        z = z_scr[n]
        c = c_scr[n]
        d_scr[n] = z - jnp.concatenate([c] * (t // LANES), axis=1)
        sp = jnp.maximum(z, 0.0) + jnp.log(1.0 + jnp.exp2(jnp.abs(z) * -LOG2_E))
        if diagonal:
            sp = jnp.where(causal, sp, 0.0)
        suffix = jnp.dot(sp.astype(bf16), keys_from, preferred_element_type=f32)
        d = d_scr[n] - suffix
        if diagonal:
            d = jnp.where(causal, d, MASKED)
        d_scr[n] = d
        c_new = c + suffix[:, 0:1]
        c_scr[n] = c_new
        return c_new

    def weigh(n, m):
        a = jnp.exp(d_scr[n]).astype(bf16)
        acc_scr[n] += jnp.dot(a, v_ref[streams[n][0], rows_of_tile(m), :],
                              preferred_element_type=f32)

    c_scr[...] = jnp.zeros(c_scr.shape, f32)
    acc_scr[...] = jnp.zeros(acc_scr.shape, f32)
    for n in range(len(streams)):
        scores(n, 0)
    for n in range(len(streams)):
        decay(n, True)
        scores(n, 1)

    def step(carry):
        m, _ = carry
        c_low = None
        for n in range(len(streams)):
            weigh(n, m)
            c_new = decay(n, False)
            c_low = c_new if c_low is None else jnp.minimum(c_low, c_new)
            scores(n, m + 2)
        return m + 1, jnp.min(c_low)

    def more_to_do(carry):
        m, c_low = carry
        return jnp.logical_and(m < i, c_low < DEAD_CARRY)

    steps_done, _ = lax.while_loop(more_to_do, step, (jnp.int32(0), jnp.float32(0.0)))

    for n in range(len(streams)):
        weigh(n, steps_done)

    for b in range(q_ref.shape[0]):
        out = jnp.where(lo_half, acc_scr[2 * b], acc_scr[2 * b + 1])
        y = _pair_group_norm(out, lo_half) * g_ref[...]
        a_z = az_ref[b]
        o_ref[b] = (y * (a_z * _sigmoid(a_z))).astype(bf16)


def _attention(q, k, v, a_z, bg_attn):
    bsz, s, d_sb = q.shape
    pairs = d_sb // LANES
    n_streams = ATTN_BATCH * (LANES // HEAD_DIM)
    tile_spec = pl.BlockSpec((ATTN_BATCH, ATTN_TILE, LANES), lambda b, p, i: (b, i, p))
    seq_spec = pl.BlockSpec((ATTN_BATCH, s, LANES), lambda b, p, i: (b, 0, p))
    return pl.pallas_call(
        _attn_kernel,
        grid=(bsz // ATTN_BATCH, pairs, s // ATTN_TILE),
        in_specs=[tile_spec, seq_spec, seq_spec, tile_spec,
                  pl.BlockSpec((1, LANES), lambda b, p, i: (0, p))],
        out_specs=tile_spec,
        out_shape=jax.ShapeDtypeStruct((bsz, s, d_sb), bf16),
        scratch_shapes=[pltpu.VMEM((n_streams, ATTN_TILE, ATTN_TILE), f32),
                        pltpu.VMEM((n_streams, ATTN_TILE, ATTN_TILE), f32),
                        pltpu.VMEM((n_streams, ATTN_TILE, LANES), f32),
                        pltpu.VMEM((n_streams, ATTN_TILE, LANES), f32)],
        compiler_params=pltpu.CompilerParams(
            dimension_semantics=("arbitrary", "arbitrary", "arbitrary"),
            vmem_limit_bytes=VMEM_LIMIT),
        name="attention",
    )(q, k, v, a_z, bg_attn)


def _out_proj_kernel(x_ref, yc_ref, ya_ref, p_ref, wo_ref, pg_ref, wpg_ref, bpg_ref, wpe_ref,
                     fg_ref, o_ref, *, d_conv, final):
    halves = [pl.ds(r * (x_ref.shape[0] // 2), x_ref.shape[0] // 2) for r in range(2)]
    x1 = [x_ref[rows, :]
          + jnp.dot(yc_ref[rows, :], wo_ref[0:d_conv, :], preferred_element_type=f32)
          + jnp.dot(ya_ref[rows, :], wo_ref[d_conv:, :], preferred_element_type=f32)
          for rows in halves]
    for rows, x1_r in zip(halves, x1):
        h = _rmsnorm(x1_r, pg_ref[...]).astype(bf16)
        gate = _sigmoid(jnp.dot(h, wpg_ref[...], preferred_element_type=f32) + bpg_ref[...])
        pe = jnp.dot(p_ref[rows, :].astype(bf16), wpe_ref[...], preferred_element_type=f32)
        x2 = x1_r + gate * pe
        o_ref[rows, :] = _rmsnorm(x2, fg_ref[...]) if final else x2


def _out_proj(x, y_c, y_a, p, w_out, ple_g, w_pg, b_pg, w_pe, final_g, final):
    n, d = x.shape
    d_conv, d_sb, ple = y_c.shape[1], y_a.shape[1], p.shape[1]
    row_spec = lambda width: pl.BlockSpec((ROW_TILE, width), lambda t: (t, 0))
    const = lambda shape: pl.BlockSpec(shape, lambda t: (0, 0), pipeline_mode=pl.Buffered(1))
    return pl.pallas_call(
        functools.partial(_out_proj_kernel, d_conv=d_conv, final=final),
        grid=(n // ROW_TILE,),
        in_specs=[row_spec(d), row_spec(d_conv), row_spec(d_sb), row_spec(ple),
                  const((d_conv + d_sb, d)), const((1, d)), const((d, d)), const((1, d)),
                  const((ple, d)), const((1, d))],
        out_specs=row_spec(d),
        out_shape=jax.ShapeDtypeStruct((n, d), f32),
        compiler_params=pltpu.CompilerParams(
            dimension_semantics=("arbitrary",), vmem_limit_bytes=VMEM_LIMIT),
        name="out_proj",
    )(x, y_c, y_a, p, w_out, ple_g, w_pg, b_pg, w_pe, final_g)


def kernel(x, p, norm_g, w_in, conv_w, conv_b, branch_g, w_out, ple_norm_g, w_pg, b_pg, w_pe, final_g):
    depth = w_in.shape[0]
    bsz, s, d = x.shape
    d_conv = conv_w.shape[2]
    d_sb = branch_g.shape[1] - d_conv
    assert s % ROW_TILE == 0 and s % ATTN_TILE == 0 and d_sb % LANES == 0 and d_conv % LANES == 0
    row = lambda vec: vec.reshape(1, -1)
    for i in range(depth):
        y_c, q, k, v, a_z = _in_proj(
            x, row(norm_g[i]), w_in[i].astype(bf16), conv_w[i], row(conv_b[i]),
            row(branch_g[i, :d_conv]), d_conv, d_sb)
        y_a = _attention(q, k, v, a_z, row(branch_g[i, d_conv:]))
        x = _out_proj(
            x.reshape(bsz * s, d), y_c.reshape(bsz * s, d_conv), y_a.reshape(bsz * s, d_sb),
            p[i].reshape(bsz * s, -1), w_out[i].astype(bf16), row(ple_norm_g[i]),
            w_pg[i].astype(bf16), row(b_pg[i]), w_pe[i].astype(bf16), row(final_g),
            final=(i == depth - 1)).reshape(bsz, s, d)
    return x
```

```python
import functools

import jax
import jax.numpy as jnp
from jax import lax
from jax.experimental import pallas as pl
from jax.experimental.pallas import tpu as pltpu

HEAD_DIM = 64
LANES = 128
CONV_WIDTH = 3
EPS = 1e-6

ROW_TILE = 1024
ATTN_TILE = 256
DEAD_CARRY = 128.0
ATTN_BATCH = 4
CARRY_ROWS = 8
VMEM_LIMIT = 48 * 1024 * 1024
LOG2_E = 1.4426950408889634

f32 = jnp.float32
bf16 = jnp.bfloat16


def _rmsnorm(x, g):
    return x * lax.rsqrt(jnp.mean(x * x, axis=-1, keepdims=True) + EPS) * g


def _sigmoid(x):
    return 1.0 / (1.0 + jnp.exp(-x))


def _pair_group_norm(y, lo_half):
    sq = y * y
    s_lo = jnp.sum(jnp.where(lo_half, sq, 0.0), axis=-1, keepdims=True)
    s_hi = jnp.sum(jnp.where(lo_half, 0.0, sq), axis=-1, keepdims=True)
    inv_lo = lax.rsqrt(s_lo * (1.0 / HEAD_DIM) + EPS)
    inv_hi = lax.rsqrt(s_hi * (1.0 / HEAD_DIM) + EPS)
    return y * jnp.where(lo_half, inv_lo, inv_hi)


def _in_proj_kernel(x_ref, g_ref, w_ref, cw_ref, cb_ref, bg_ref,
                    yc_ref, q_ref, k_ref, v_ref, az_ref, u_scr, *, layer, d_conv, d_sb):
    rows = x_ref.shape[1]
    this_layer = slice(layer, layer + 1)

    @pl.when(pl.program_id(1) == 0)
    def _():
        u_scr[0:CARRY_ROWS, :] = jnp.zeros((CARRY_ROWS, d_conv), f32)

    h = _rmsnorm(x_ref[0], g_ref[this_layer, :])

    def proj(col0, width):
        return jnp.dot(h, w_ref[:, col0:col0 + width], preferred_element_type=f32)

    c_b = proj(0, d_conv)
    u = proj(d_conv, d_conv) * proj(2 * d_conv, d_conv)
    c_z = proj(3 * d_conv, d_conv)
    u_scr[CARRY_ROWS:CARRY_ROWS + rows, :] = u
    lo_half = lax.broadcasted_iota(jnp.int32, (1, LANES), 1) < HEAD_DIM

    def conv_block(j):
        sl = slice(j * LANES, (j + 1) * LANES)
        conv = cb_ref[this_layer, sl]
        for tap in range(CONV_WIDTH):
            back = CONV_WIDTH - 1 - tap
            conv = conv + u_scr[CARRY_ROWS - back:CARRY_ROWS - back + rows, sl] * cw_ref[tap:tap + 1, sl]
        y = _pair_group_norm(c_b[:, sl] * conv, lo_half) * bg_ref[this_layer, sl]
        gate = c_z[:, sl]
        yc_ref[0, :, sl] = (y * (gate * _sigmoid(gate))).astype(bf16)

    attn_outs = ((q_ref, HEAD_DIM ** -0.5), (k_ref, None), (v_ref, None), (az_ref, None))

    def attn_block(j):
        ref, scale = attn_outs[j]
        y = proj(4 * d_conv + j * d_sb, d_sb)
        ref[0] = (y if scale is None else y * scale).astype(ref.dtype)

    n_conv_blocks = d_conv // LANES
    for j in range(max(n_conv_blocks, len(attn_outs))):
        if j < n_conv_blocks:
            conv_block(j)
        if j < len(attn_outs):
            attn_block(j)
    u_scr[0:CARRY_ROWS, :] = u_scr[rows:rows + CARRY_ROWS, :]


def _in_proj(layer, x, norm_g, w_in, conv_w, conv_b, branch_g, d_conv, d_sb):
    bsz, s, d = x.shape
    depth, _, n_in = w_in.shape
    row_spec = lambda width: pl.BlockSpec((1, ROW_TILE, width), lambda b, t: (b, t, 0))
    const = lambda shape: pl.BlockSpec(shape, lambda b, t: (0, 0), pipeline_mode=pl.Buffered(1))
    of_layer = lambda shape: pl.BlockSpec((None,) + shape, lambda b, t: (layer, 0, 0),
                                          pipeline_mode=pl.Buffered(1))
    return pl.pallas_call(
        functools.partial(_in_proj_kernel, layer=layer, d_conv=d_conv, d_sb=d_sb),
        grid=(bsz, s // ROW_TILE),
        in_specs=[row_spec(d), const((depth, d)), of_layer((d, n_in)),
                  of_layer((CONV_WIDTH, d_conv)), const((depth, d_conv)), const((depth, d_conv))],
        out_specs=[row_spec(d_conv), row_spec(d_sb), row_spec(d_sb), row_spec(d_sb), row_spec(d_sb)],
        out_shape=[jax.ShapeDtypeStruct((bsz, s, d_conv), bf16),
                   jax.ShapeDtypeStruct((bsz, s, d_sb), bf16),
                   jax.ShapeDtypeStruct((bsz, s, d_sb), bf16),
                   jax.ShapeDtypeStruct((bsz, s, d_sb), bf16),
                   jax.ShapeDtypeStruct((bsz, s, d_sb), f32)],
        scratch_shapes=[pltpu.VMEM((ROW_TILE + CARRY_ROWS, d_conv), f32)],
        compiler_params=pltpu.CompilerParams(
            dimension_semantics=("arbitrary", "arbitrary"), vmem_limit_bytes=VMEM_LIMIT),
        name="in_proj",
    )(x, norm_g, w_in, conv_w, conv_b, branch_g)


MASKED = -1e30


def _attn_kernel(q_ref, k_ref, v_ref, az_ref, g_ref, o_ref, z_scr, d_scr, c_scr, acc_scr, *, layer):
    t = ATTN_TILE
    i = pl.program_id(2)
    lo_half = lax.broadcasted_iota(jnp.int32, (1, LANES), 1) < HEAD_DIM
    key_pos = lax.broadcasted_iota(jnp.int32, (t, t), 1)
    query_pos = lax.broadcasted_iota(jnp.int32, (t, t), 0)
    causal = key_pos < query_pos
    keys_from = (query_pos >= key_pos).astype(bf16)
    streams = []
    for b in range(q_ref.shape[0]):
        q = q_ref[b]
        zero = jnp.zeros_like(q)
        streams.append((b, jnp.where(lo_half, q, zero)))
        streams.append((b, jnp.where(lo_half, zero, q)))

    def rows_of_tile(m):
        return pl.ds(pl.multiple_of(jnp.maximum(i - m, 0) * t, t), t)

    def scores(n, m):
        b, q_h = streams[n]
        z_scr[n] = lax.dot_general(q_h, k_ref[b, rows_of_tile(m), :], (((1,), (1,)), ((), ())),
                                   preferred_element_type=f32)

    def decay(n, diagonal=False, exists=None):
        z = z_scr[n]
        c = c_scr[n]
        d_scr[n] = z - jnp.concatenate([c] * (t // LANES), axis=1)
        sp = jnp.maximum(z, 0.0) + jnp.log(1.0 + jnp.exp2(jnp.abs(z) * -LOG2_E))
        if diagonal:
            sp = jnp.where(causal, sp, 0.0)
        suffix = jnp.dot(sp.astype(bf16), keys_from, preferred_element_type=f32)
        d = d_scr[n] - suffix
        if diagonal:
            d = jnp.where(causal, d, MASKED)
        if exists is not None:
            d = jnp.where(exists, d, MASKED)
        d_scr[n] = d
        c_new = c + suffix[:, 0:1]
        c_scr[n] = c_new
        return c_new

    def weigh(n, m):
        a = jnp.exp(d_scr[n]).astype(bf16)
        acc_scr[n] += jnp.dot(a, v_ref[streams[n][0], rows_of_tile(m), :],
                              preferred_element_type=f32)

    def step(carry, exists=None):
        m, _ = carry
        c_low = None
        for n in range(len(streams)):
            weigh(n, m)
            c_new = decay(n, exists=exists)
            c_low = c_new if c_low is None else jnp.minimum(c_low, c_new)
            scores(n, m + 2)
        return m + 1, jnp.min(c_low)

    def more_to_do(carry):
        m, c_low = carry
        return jnp.logical_and(m < i, c_low < DEAD_CARRY)

    c_scr[...] = jnp.zeros(c_scr.shape, f32)
    acc_scr[...] = jnp.zeros(acc_scr.shape, f32)
    for n in range(len(streams)):
        scores(n, 0)
    for n in range(len(streams)):
        decay(n, diagonal=True)
        scores(n, 1)
    first = step((jnp.int32(0), None), exists=i >= 1)
    steps_done, _ = lax.while_loop(more_to_do, step, first)

    for n in range(len(streams)):
        weigh(n, steps_done)

    for b in range(q_ref.shape[0]):
        out = jnp.where(lo_half, acc_scr[2 * b], acc_scr[2 * b + 1])
        y = _pair_group_norm(out, lo_half) * g_ref[layer:layer + 1, :]
        a_z = az_ref[b]
        o_ref[b] = (y * (a_z * _sigmoid(a_z))).astype(bf16)


def _attention(layer, q, k, v, a_z, branch_g):
    bsz, s, d_sb = q.shape
    depth, d_mix = branch_g.shape
    first_attn_block = (d_mix - d_sb) // LANES
    pairs = d_sb // LANES
    n_streams = ATTN_BATCH * (LANES // HEAD_DIM)
    tile_spec = pl.BlockSpec((ATTN_BATCH, ATTN_TILE, LANES), lambda b, p, i: (b, i, p))
    seq_spec = pl.BlockSpec((ATTN_BATCH, s, LANES), lambda b, p, i: (b, 0, p))
    return pl.pallas_call(
        functools.partial(_attn_kernel, layer=layer),
        grid=(bsz // ATTN_BATCH, pairs, s // ATTN_TILE),
        in_specs=[tile_spec, seq_spec, seq_spec, tile_spec,
                  pl.BlockSpec((depth, LANES), lambda b, p, i: (0, first_attn_block + p))],
        out_specs=tile_spec,
        out_shape=jax.ShapeDtypeStruct((bsz, s, d_sb), bf16),
        scratch_shapes=[pltpu.VMEM((n_streams, ATTN_TILE, ATTN_TILE), f32),
                        pltpu.VMEM((n_streams, ATTN_TILE, ATTN_TILE), f32),
                        pltpu.VMEM((n_streams, ATTN_TILE, LANES), f32),
                        pltpu.VMEM((n_streams, ATTN_TILE, LANES), f32)],
        compiler_params=pltpu.CompilerParams(
            dimension_semantics=("arbitrary", "arbitrary", "arbitrary"),
            vmem_limit_bytes=VMEM_LIMIT),
        name="attention",
    )(q, k, v, a_z, branch_g)


def _out_proj_kernel(x_ref, yc_ref, ya_ref, p_ref, wo_ref, pg_ref, wpg_ref, bpg_ref, wpe_ref,
                     fg_ref, o_ref, *, layer, d_conv, final):
    this_layer = slice(layer, layer + 1)
    halves = [pl.ds(r * (x_ref.shape[0] // 2), x_ref.shape[0] // 2) for r in range(2)]
    x1 = [x_ref[rows, :]
          + jnp.dot(yc_ref[rows, :], wo_ref[0:d_conv, :], preferred_element_type=f32)
          + jnp.dot(ya_ref[rows, :], wo_ref[d_conv:, :], preferred_element_type=f32)
          for rows in halves]
    for rows, x1_r in zip(halves, x1):
        h = _rmsnorm(x1_r, pg_ref[this_layer, :])
        gate = _sigmoid(jnp.dot(h, wpg_ref[...], preferred_element_type=f32) + bpg_ref[this_layer, :])
        pe = jnp.dot(p_ref[rows, :], wpe_ref[...], preferred_element_type=f32)
        x2 = x1_r + gate * pe
        o_ref[rows, :] = _rmsnorm(x2, fg_ref[...]) if final else x2


def _out_proj(layer, x, y_c, y_a, p, w_out, ple_g, w_pg, b_pg, w_pe, final_g, final):
    n, d = x.shape
    d_conv, d_sb = y_c.shape[1], y_a.shape[1]
    depth, _, ple = p.shape
    row_spec = lambda width: pl.BlockSpec((ROW_TILE, width), lambda t: (t, 0))
    const = lambda shape: pl.BlockSpec(shape, lambda t: (0, 0), pipeline_mode=pl.Buffered(1))
    of_layer = lambda shape: pl.BlockSpec((None,) + shape, lambda t: (layer, 0, 0),
                                          pipeline_mode=pl.Buffered(1))
    return pl.pallas_call(
        functools.partial(_out_proj_kernel, layer=layer, d_conv=d_conv, final=final),
        grid=(n // ROW_TILE,),
        in_specs=[row_spec(d), row_spec(d_conv), row_spec(d_sb),
                  pl.BlockSpec((None, ROW_TILE, ple), lambda t: (layer, t, 0)),
                  of_layer((d_conv + d_sb, d)), const((depth, d)), of_layer((d, d)), const((depth, d)),
                  of_layer((ple, d)), const((1, d))],
        out_specs=row_spec(d),
        out_shape=jax.ShapeDtypeStruct((n, d), f32),
        compiler_params=pltpu.CompilerParams(
            dimension_semantics=("arbitrary",), vmem_limit_bytes=VMEM_LIMIT),
        name="out_proj",
    )(x, y_c, y_a, p, w_out, ple_g, w_pg, b_pg, w_pe, final_g)


def kernel(x, p, norm_g, w_in, conv_w, conv_b, branch_g, w_out, ple_norm_g, w_pg, b_pg, w_pe, final_g):
    depth = w_in.shape[0]
    bsz, s, d = x.shape
    d_conv = conv_w.shape[2]
    d_sb = branch_g.shape[1] - d_conv
    assert s % ROW_TILE == 0 and s % ATTN_TILE == 0 and d_sb % LANES == 0 and d_conv % LANES == 0
    assert bsz % ATTN_BATCH == 0
    w_out = w_out.astype(bf16)
    p = p.reshape(depth, bsz * s, -1)
    final_g = final_g.reshape(1, -1)
    for i in range(depth):
        y_c, q, k, v, a_z = _in_proj(i, x, norm_g, w_in, conv_w, conv_b, branch_g, d_conv, d_sb)
        y_a = _attention(i, q, k, v, a_z, branch_g)
        x = _out_proj(
            i, x.reshape(bsz * s, d), y_c.reshape(bsz * s, d_conv), y_a.reshape(bsz * s, d_sb),
            p, w_out, ple_norm_g, w_pg, b_pg, w_pe, final_g,
            final=(i == depth - 1)).reshape(bsz, s, d)
    return x
```

```python
import functools

import jax
import jax.numpy as jnp
from jax import lax
from jax.experimental import pallas as pl
from jax.experimental.pallas import tpu as pltpu

HEAD_DIM = 64
LANES = 128
CONV_WIDTH = 3
EPS = 1e-6

ROW_TILE = 1024
ATTN_TILE = 256
DEAD_CARRY = 128.0
SCORES_AHEAD = 2
ATTN_PAIRS = 2
ATTN_BATCH = 4
CARRY_ROWS = 8
VMEM_LIMIT = 48 * 1024 * 1024
LOG2_E = 1.4426950408889634

f32 = jnp.float32
bf16 = jnp.bfloat16


def _rmsnorm(x, g):
    return x * lax.rsqrt(jnp.mean(x * x, axis=-1, keepdims=True) + EPS) * g


def _sigmoid(x):
    return 1.0 / (1.0 + jnp.exp(-x))


def _pair_group_norm(y, lo_half):
    sq = y * y
    s_lo = jnp.sum(jnp.where(lo_half, sq, 0.0), axis=-1, keepdims=True)
    s_hi = jnp.sum(jnp.where(lo_half, 0.0, sq), axis=-1, keepdims=True)
    inv_lo = lax.rsqrt(s_lo * (1.0 / HEAD_DIM) + EPS)
    inv_hi = lax.rsqrt(s_hi * (1.0 / HEAD_DIM) + EPS)
    return y * jnp.where(lo_half, inv_lo, inv_hi)


def _in_proj_kernel(x_ref, g_ref, w_ref, cw_ref, cb_ref, bg_ref,
                    yc_ref, q_ref, k_ref, v_ref, az_ref, u_scr, *, layer, d_conv, d_sb):
    rows = x_ref.shape[1]
    this_layer = slice(layer, layer + 1)

    @pl.when(pl.program_id(1) == 0)
    def _():
        u_scr[0:CARRY_ROWS, :] = jnp.zeros((CARRY_ROWS, d_conv), f32)

    h = _rmsnorm(x_ref[0], g_ref[this_layer, :])

    def proj(col0, width):
        return jnp.dot(h, w_ref[:, col0:col0 + width], preferred_element_type=f32)

    c_b = proj(0, d_conv)
    u = proj(d_conv, d_conv) * proj(2 * d_conv, d_conv)
    c_z = proj(3 * d_conv, d_conv)
    u_scr[CARRY_ROWS:CARRY_ROWS + rows, :] = u
    lo_half = lax.broadcasted_iota(jnp.int32, (1, LANES), 1) < HEAD_DIM

    def conv_block(j):
        sl = slice(j * LANES, (j + 1) * LANES)
        conv = cb_ref[this_layer, sl]
        for tap in range(CONV_WIDTH):
            back = CONV_WIDTH - 1 - tap
            conv = conv + u_scr[CARRY_ROWS - back:CARRY_ROWS - back + rows, sl] * cw_ref[tap:tap + 1, sl]
        y = _pair_group_norm(c_b[:, sl] * conv, lo_half) * bg_ref[this_layer, sl]
        gate = c_z[:, sl]
        yc_ref[0, :, sl] = (y * (gate * _sigmoid(gate))).astype(bf16)

    attn_outs = ((q_ref, HEAD_DIM ** -0.5), (k_ref, None), (v_ref, None), (az_ref, None))

    def attn_block(j):
        ref, scale = attn_outs[j]
        y = proj(4 * d_conv + j * d_sb, d_sb)
        ref[0] = (y if scale is None else y * scale).astype(ref.dtype)

    n_conv_blocks = d_conv // LANES
    for j in range(max(n_conv_blocks, len(attn_outs))):
        if j < n_conv_blocks:
            conv_block(j)
        if j < len(attn_outs):
            attn_block(j)
    u_scr[0:CARRY_ROWS, :] = u_scr[rows:rows + CARRY_ROWS, :]


def _in_proj(layer, x, norm_g, w_in, conv_w, conv_b, branch_g, d_conv, d_sb):
    bsz, s, d = x.shape
    depth, _, n_in = w_in.shape
    row_spec = lambda width: pl.BlockSpec((1, ROW_TILE, width), lambda b, t: (b, t, 0))
    const = lambda shape: pl.BlockSpec(shape, lambda b, t: (0, 0), pipeline_mode=pl.Buffered(1))
    of_layer = lambda shape: pl.BlockSpec((None,) + shape, lambda b, t: (layer, 0, 0),
                                          pipeline_mode=pl.Buffered(1))
    return pl.pallas_call(
        functools.partial(_in_proj_kernel, layer=layer, d_conv=d_conv, d_sb=d_sb),
        grid=(bsz, s // ROW_TILE),
        in_specs=[row_spec(d), const((depth, d)), of_layer((d, n_in)),
                  of_layer((CONV_WIDTH, d_conv)), const((depth, d_conv)), const((depth, d_conv))],
        out_specs=[row_spec(d_conv), row_spec(d_sb), row_spec(d_sb), row_spec(d_sb), row_spec(d_sb)],
        out_shape=[jax.ShapeDtypeStruct((bsz, s, d_conv), bf16),
                   jax.ShapeDtypeStruct((bsz, s, d_sb), bf16),
                   jax.ShapeDtypeStruct((bsz, s, d_sb), bf16),
                   jax.ShapeDtypeStruct((bsz, s, d_sb), bf16),
                   jax.ShapeDtypeStruct((bsz, s, d_sb), f32)],
        scratch_shapes=[pltpu.VMEM((ROW_TILE + CARRY_ROWS, d_conv), f32)],
        compiler_params=pltpu.CompilerParams(
            dimension_semantics=("arbitrary", "arbitrary"), vmem_limit_bytes=VMEM_LIMIT),
        name="in_proj",
    )(x, norm_g, w_in, conv_w, conv_b, branch_g)


MASKED = -1e30


def _attn_kernel(q_ref, k_ref, v_ref, az_ref, g_ref, o_ref, z_scr, d_scr, c_scr, acc_scr, *, layer):
    t, half = ATTN_TILE, ATTN_TILE // 2
    top, bottom, all_rows = slice(0, half), slice(half, t), slice(0, t)
    i = pl.program_id(2)
    lo_half = lax.broadcasted_iota(jnp.int32, (1, LANES), 1) < HEAD_DIM

    def triangle(rows, cols, row0):
        return (lax.broadcasted_iota(jnp.int32, (rows, cols), 1)
                < lax.broadcasted_iota(jnp.int32, (rows, cols), 0) + row0)

    keys_from = {w: triangle(w, w, 1).astype(bf16) for w in (half, t)}
    streams = []
    for b in range(q_ref.shape[0]):
        for p in range(q_ref.shape[2] // LANES):
            lanes = slice(p * LANES, (p + 1) * LANES)
            q = q_ref[b, :, lanes]
            zero = jnp.zeros_like(q)
            streams.append((b, lanes, jnp.where(lo_half, q, zero)))
            streams.append((b, lanes, jnp.where(lo_half, zero, q)))
    n_streams = len(streams)

    def keys_of_tile(m, width=t):
        return pl.ds(pl.multiple_of(jnp.maximum(i - m, 0) * t, t), width)

    def scores(n, m):
        b, lanes, q_h = streams[n]
        z_scr[n] = lax.dot_general(q_h, k_ref[b, keys_of_tile(m), lanes], (((1,), (1,)), ((), ())),
                                   preferred_element_type=f32)

    def decay(n, rows=all_rows, width=t, mask=None, exists=None):
        z = z_scr[n, rows, 0:width]
        c = c_scr[n, rows, :]
        d_scr[n, rows, 0:width] = z - jnp.concatenate([c] * (width // LANES), axis=1)
        sp = jnp.maximum(z, 0.0) + jnp.log(1.0 + jnp.exp2(jnp.abs(z) * -LOG2_E))
        if mask is not None:
            sp = jnp.where(mask, sp, 0.0)
        suffix = jnp.dot(sp.astype(bf16), keys_from[width], preferred_element_type=f32)
        d = d_scr[n, rows, 0:width] - suffix
        if mask is not None:
            d = jnp.where(mask, d, MASKED)
        if exists is not None:
            d = jnp.where(exists, d, MASKED)
        d_scr[n, rows, 0:width] = d
        c_new = c + suffix[:, 0:1]
        c_scr[n, rows, :] = c_new
        return c_new

    def weigh(n, m, rows=all_rows, width=t):
        b, lanes, _ = streams[n]
        a = jnp.exp(d_scr[n, rows, 0:width]).astype(bf16)
        acc_scr[n, rows, :] += jnp.dot(a, v_ref[b, keys_of_tile(m, width), lanes],
                                       preferred_element_type=f32)

    def lowest(values):
        return jnp.min(functools.reduce(jnp.minimum, values))

    def step(carry):
        m, _ = carry
        c_new = []
        for n in range(n_streams):
            weigh(n, m)
            c_new.append(decay(n))
            scores(n, m + 2)
        return m + 1, lowest(c_new)

    def first_step(top_only):
        exists = i >= 1
        c_new = []
        for n in range(n_streams):
            weigh(n, 0, top, half)
            weigh(n, 0, bottom)
            if top_only:
                c_new.append(decay(n, top, exists=exists))
                d_scr[n, bottom, :] = jnp.full((half, t), MASKED, f32)
            else:
                c_new.append(decay(n, exists=exists))
            scores(n, 2)
        return jnp.int32(1), lowest(c_new)

    def more_to_do(carry):
        m, c_low = carry
        return jnp.logical_and(m < i, c_low < DEAD_CARRY)

    c_scr[...] = jnp.zeros(c_scr.shape, f32)
    acc_scr[...] = jnp.zeros(acc_scr.shape, f32)
    for n in range(SCORES_AHEAD):
        scores(n, 0)
    c_bottom = []
    for n in range(n_streams):
        if n + SCORES_AHEAD < n_streams:
            scores(n + SCORES_AHEAD, 0)
        decay(n, top, half, mask=triangle(half, half, 0))
        c_bottom.append(decay(n, bottom, mask=triangle(half, t, half)))
        scores(n, 1)
    bottom_dead = lowest(c_bottom) >= DEAD_CARRY
    first = lax.cond(bottom_dead, functools.partial(first_step, True),
                     functools.partial(first_step, False))
    steps_done, _ = lax.while_loop(more_to_do, step, first)

    def last_weigh(rows):
        for n in range(n_streams):
            weigh(n, steps_done, rows)

    lax.cond(bottom_dead, functools.partial(last_weigh, top), functools.partial(last_weigh, all_rows))

    for n in range(0, n_streams, 2):
        b, lanes, _ = streams[n]
        out = jnp.where(lo_half, acc_scr[n], acc_scr[n + 1])
        y = _pair_group_norm(out, lo_half) * g_ref[layer:layer + 1, lanes]
        a_z = az_ref[b, :, lanes]
        o_ref[b, :, lanes] = (y * (a_z * _sigmoid(a_z))).astype(bf16)


def _attention(layer, q, k, v, a_z, branch_g):
    bsz, s, d_sb = q.shape
    depth, d_mix = branch_g.shape
    width = ATTN_PAIRS * LANES
    first_attn_block = (d_mix - d_sb) // width
    n_streams = ATTN_BATCH * (width // HEAD_DIM)
    tile_spec = pl.BlockSpec((ATTN_BATCH, ATTN_TILE, width), lambda b, p, i: (b, i, p))
    seq_spec = pl.BlockSpec((ATTN_BATCH, s, width), lambda b, p, i: (b, 0, p),
                            pipeline_mode=pl.Buffered(1))
    return pl.pallas_call(
        functools.partial(_attn_kernel, layer=layer),
        grid=(bsz // ATTN_BATCH, d_sb // width, s // ATTN_TILE),
        in_specs=[tile_spec, seq_spec, seq_spec, tile_spec,
                  pl.BlockSpec((depth, width), lambda b, p, i: (0, first_attn_block + p))],
        out_specs=tile_spec,
        out_shape=jax.ShapeDtypeStruct((bsz, s, d_sb), bf16),
        scratch_shapes=[pltpu.VMEM((n_streams, ATTN_TILE, ATTN_TILE), f32),
                        pltpu.VMEM((n_streams, ATTN_TILE, ATTN_TILE), f32),
                        pltpu.VMEM((n_streams, ATTN_TILE, LANES), f32),
                        pltpu.VMEM((n_streams, ATTN_TILE, LANES), f32)],
        compiler_params=pltpu.CompilerParams(
            dimension_semantics=("arbitrary", "arbitrary", "arbitrary"),
            vmem_limit_bytes=VMEM_LIMIT),
        name="attention",
    )(q, k, v, a_z, branch_g)


def _out_proj_kernel(x_ref, yc_ref, ya_ref, p_ref, wo_ref, pg_ref, wpg_ref, bpg_ref, wpe_ref,
                     fg_ref, o_ref, *, layer, d_conv, final):
    this_layer = slice(layer, layer + 1)
    halves = [pl.ds(r * (x_ref.shape[0] // 2), x_ref.shape[0] // 2) for r in range(2)]
    x1 = [x_ref[rows, :]
          + jnp.dot(yc_ref[rows, :], wo_ref[0:d_conv, :], preferred_element_type=f32)
          + jnp.dot(ya_ref[rows, :], wo_ref[d_conv:, :], preferred_element_type=f32)
          for rows in halves]
    for rows, x1_r in zip(halves, x1):
        h = _rmsnorm(x1_r, pg_ref[this_layer, :])
        gate = _sigmoid(jnp.dot(h, wpg_ref[...], preferred_element_type=f32) + bpg_ref[this_layer, :])
        pe = jnp.dot(p_ref[rows, :], wpe_ref[...], preferred_element_type=f32)
        x2 = x1_r + gate * pe
        o_ref[rows, :] = _rmsnorm(x2, fg_ref[...]) if final else x2


def _out_proj(layer, x, y_c, y_a, p, w_out, ple_g, w_pg, b_pg, w_pe, final_g, final):
    n, d = x.shape
    d_conv, d_sb = y_c.shape[1], y_a.shape[1]
    depth, _, ple = p.shape
    row_spec = lambda width: pl.BlockSpec((ROW_TILE, width), lambda t: (t, 0))
    const = lambda shape: pl.BlockSpec(shape, lambda t: (0, 0), pipeline_mode=pl.Buffered(1))
    of_layer = lambda shape: pl.BlockSpec((None,) + shape, lambda t: (layer, 0, 0),
                                          pipeline_mode=pl.Buffered(1))
    return pl.pallas_call(
        functools.partial(_out_proj_kernel, layer=layer, d_conv=d_conv, final=final),
        grid=(n // ROW_TILE,),
        in_specs=[row_spec(d), row_spec(d_conv), row_spec(d_sb),
                  pl.BlockSpec((None, ROW_TILE, ple), lambda t: (layer, t, 0)),
                  of_layer((d_conv + d_sb, d)), const((depth, d)), of_layer((d, d)), const((depth, d)),
                  of_layer((ple, d)), const((1, d))],
        out_specs=row_spec(d),
        out_shape=jax.ShapeDtypeStruct((n, d), f32),
        compiler_params=pltpu.CompilerParams(
            dimension_semantics=("arbitrary",), vmem_limit_bytes=VMEM_LIMIT),
        name="out_proj",
    )(x, y_c, y_a, p, w_out, ple_g, w_pg, b_pg, w_pe, final_g)


def kernel(x, p, norm_g, w_in, conv_w, conv_b, branch_g, w_out, ple_norm_g, w_pg, b_pg, w_pe, final_g):
    depth = w_in.shape[0]
    bsz, s, d = x.shape
    d_conv = conv_w.shape[2]
    d_sb = branch_g.shape[1] - d_conv
    assert s % ROW_TILE == 0 and s % ATTN_TILE == 0 and d_sb % LANES == 0 and d_conv % LANES == 0
    assert bsz % ATTN_BATCH == 0
    w_out = w_out.astype(bf16)
    p = p.reshape(depth, bsz * s, -1)
    final_g = final_g.reshape(1, -1)
    for i in range(depth):
        y_c, q, k, v, a_z = _in_proj(i, x, norm_g, w_in, conv_w, conv_b, branch_g, d_conv, d_sb)
        y_a = _attention(i, q, k, v, a_z, branch_g)
        x = _out_proj(
            i, x.reshape(bsz * s, d), y_c.reshape(bsz * s, d_conv), y_a.reshape(bsz * s, d_sb),
            p, w_out, ple_norm_g, w_pg, b_pg, w_pe, final_g,
            final=(i == depth - 1)).reshape(bsz, s, d)
    return x
```

```python
import functools

import jax
import jax.numpy as jnp
from jax import lax
from jax.experimental import pallas as pl
from jax.experimental.pallas import tpu as pltpu

HEAD_DIM = 64
LANES = 128
CONV_WIDTH = 3
EPS = 1e-6

ROW_TILE = 1024
ATTN_TILE = 256
DEAD_CARRY = 128.0
ATTN_BATCH = 4
CARRY_ROWS = 8
VMEM_LIMIT = 48 * 1024 * 1024
LOG2_E = 1.4426950408889634

f32 = jnp.float32
bf16 = jnp.bfloat16


def _rmsnorm(x, g):
    return x * lax.rsqrt(jnp.mean(x * x, axis=-1, keepdims=True) + EPS) * g


def _sigmoid(x):
    return 1.0 / (1.0 + jnp.exp(-x))


def _silu(x):
    h = 0.5 * x
    return h + h * jnp.tanh(h)


def _pair_group_norm(y, lo_half):
    sq = y * y
    s_lo = jnp.sum(jnp.where(lo_half, sq, 0.0), axis=-1, keepdims=True)
    s_hi = jnp.sum(jnp.where(lo_half, 0.0, sq), axis=-1, keepdims=True)
    inv_lo = lax.rsqrt(s_lo * (1.0 / HEAD_DIM) + EPS)
    inv_hi = lax.rsqrt(s_hi * (1.0 / HEAD_DIM) + EPS)
    return y * jnp.where(lo_half, inv_lo, inv_hi)


def _in_proj_kernel(x_ref, g_ref, w_ref, cw_ref, cb_ref, bg_ref,
                    yc_ref, q_ref, k_ref, v_ref, az_ref, u_scr, *, layer, d_conv, d_sb):
    rows = x_ref.shape[1]
    this_layer = slice(layer, layer + 1)

    @pl.when(pl.program_id(1) == 0)
    def _():
        u_scr[0:CARRY_ROWS, :] = jnp.zeros((CARRY_ROWS, d_conv), f32)

    h = _rmsnorm(x_ref[0], g_ref[this_layer, :])

    def proj(col0, width):
        return jnp.dot(h, w_ref[:, col0:col0 + width], preferred_element_type=f32)

    c_b = proj(0, d_conv)
    u_scr[CARRY_ROWS:CARRY_ROWS + rows, :] = proj(d_conv, d_conv) * proj(2 * d_conv, d_conv)
    c_z = proj(3 * d_conv, d_conv)
    lo_half = lax.broadcasted_iota(jnp.int32, (1, LANES), 1) < HEAD_DIM
    for j in range(d_conv // LANES):
        sl = slice(j * LANES, (j + 1) * LANES)
        conv = cb_ref[this_layer, sl]
        for tap in range(CONV_WIDTH):
            back = CONV_WIDTH - 1 - tap
            conv = conv + u_scr[CARRY_ROWS - back:CARRY_ROWS - back + rows, sl] * cw_ref[tap:tap + 1, sl]
        y = _pair_group_norm(c_b[:, sl] * conv, lo_half) * bg_ref[this_layer, sl]
        yc_ref[0, :, sl] = (y * _silu(c_z[:, sl])).astype(bf16)
    u_scr[0:CARRY_ROWS, :] = u_scr[rows:rows + CARRY_ROWS, :]

    base = 4 * d_conv
    q_ref[0] = (proj(base, d_sb) * (HEAD_DIM ** -0.5)).astype(bf16)
    k_ref[0] = proj(base + d_sb, d_sb).astype(bf16)
    v_ref[0] = proj(base + 2 * d_sb, d_sb).astype(bf16)
    az_ref[0] = proj(base + 3 * d_sb, d_sb)


def _in_proj(layer, x, norm_g, w_in, conv_w, conv_b, branch_g, d_conv, d_sb):
    bsz, s, d = x.shape
    depth, _, n_in = w_in.shape
    row_spec = lambda width: pl.BlockSpec((1, ROW_TILE, width), lambda b, t: (b, t, 0))
    const = lambda shape: pl.BlockSpec(shape, lambda b, t: (0, 0), pipeline_mode=pl.Buffered(1))
    of_layer = lambda shape: pl.BlockSpec((None,) + shape, lambda b, t: (layer, 0, 0),
                                          pipeline_mode=pl.Buffered(1))
    return pl.pallas_call(
        functools.partial(_in_proj_kernel, layer=layer, d_conv=d_conv, d_sb=d_sb),
        grid=(bsz, s // ROW_TILE),
        in_specs=[row_spec(d), const((depth, d)), of_layer((d, n_in)),
                  of_layer((CONV_WIDTH, d_conv)), const((depth, d_conv)), const((depth, d_conv))],
        out_specs=[row_spec(d_conv), row_spec(d_sb), row_spec(d_sb), row_spec(d_sb), row_spec(d_sb)],
        out_shape=[jax.ShapeDtypeStruct((bsz, s, d_conv), bf16),
                   jax.ShapeDtypeStruct((bsz, s, d_sb), bf16),
                   jax.ShapeDtypeStruct((bsz, s, d_sb), bf16),
                   jax.ShapeDtypeStruct((bsz, s, d_sb), bf16),
                   jax.ShapeDtypeStruct((bsz, s, d_sb), f32)],
        scratch_shapes=[pltpu.VMEM((ROW_TILE + CARRY_ROWS, d_conv), f32)],
        compiler_params=pltpu.CompilerParams(
            dimension_semantics=("arbitrary", "arbitrary"), vmem_limit_bytes=VMEM_LIMIT),
        name="in_proj",
    )(x, norm_g, w_in, conv_w, conv_b, branch_g)


MASKED = -1e30


def _attn_kernel(q_ref, k_ref, v_ref, o_ref, z_scr, d_scr, c_scr, acc_scr):
    t = ATTN_TILE
    i = pl.program_id(2)
    lo_half = lax.broadcasted_iota(jnp.int32, (1, LANES), 1) < HEAD_DIM
    key_pos = lax.broadcasted_iota(jnp.int32, (t, t), 1)
    query_pos = lax.broadcasted_iota(jnp.int32, (t, t), 0)
    causal = key_pos < query_pos
    keys_from = (query_pos >= key_pos).astype(bf16)
    streams = []
    for b in range(q_ref.shape[0]):
        q = q_ref[b]
        zero = jnp.zeros_like(q)
        streams.append((b, jnp.where(lo_half, q, zero)))
        streams.append((b, jnp.where(lo_half, zero, q)))

    def rows_of_tile(m):
        return pl.ds(pl.multiple_of(jnp.maximum(i - m, 0) * t, t), t)

    def scores(n, m):
        b, q_h = streams[n]
        z_scr[n] = lax.dot_general(q_h, k_ref[b, rows_of_tile(m), :], (((1,), (1,)), ((), ())),
                                   preferred_element_type=f32)

    def decay(n, diagonal=False, exists=None):
        z = z_scr[n]
        c = c_scr[n]
        d_scr[n] = z - jnp.concatenate([c] * (t // LANES), axis=1)
        sp =jnp.maximum(z, 0.0) + jnp.log(1.0 + jnp.exp2(jnp.abs(z) * -LOG2_E))
        if diagonal:
            sp = jnp.where(causal, sp, 0.0)
        suffix = jnp.dot(sp.astype(bf16), keys_from, preferred_element_type=f32)
        d = d_scr[n] - suffix
        if diagonal:
            d = jnp.where(causal, d, MASKED)
        if exists is not None:
            d = jnp.where(exists, d, MASKED)
        d_scr[n] = d
        c_new = c + suffix[:, 0:1]
        c_scr[n] = c_new
        return c_new

    def weigh(n, m):
        a = jnp.exp(d_scr[n]).astype(bf16)
        acc_scr[n] += jnp.dot(a, v_ref[streams[n][0], rows_of_tile(m), :],
                              preferred_element_type=f32)

    def step(carry, exists=None):
        m, _ = carry
        c_low = None
        for n in range(len(streams)):
            weigh(n, m)
            c_new = decay(n, exists=exists)
            c_low = c_new if c_low is None else jnp.minimum(c_low, c_new)
            scores(n, m + 2)
        return m + 1, jnp.min(c_low)

    def more_to_do(carry):
        m, c_low = carry
        return jnp.logical_and(m < i, c_low < DEAD_CARRY)

    c_scr[...] = jnp.zeros(c_scr.shape, f32)
    acc_scr[...] = jnp.zeros(acc_scr.shape, f32)
    for n in range(len(streams)):
        scores(n, 0)
    for n in range(len(streams)):
        decay(n, diagonal=True)
        scores(n, 1)
    first = step((jnp.int32(0), None), exists=i >= 1)
    steps_done, _ = lax.while_loop(more_to_do, step, first)

    for n in range(len(streams)):
        weigh(n, steps_done)

    for b in range(q_ref.shape[0]):
        o_ref[b] = jnp.where(lo_half, acc_scr[2 * b], acc_scr[2 * b + 1])


def _attention(q, k, v):
    bsz, s, d_sb = q.shape
    pairs = d_sb // LANES
    n_streams = ATTN_BATCH * (LANES // HEAD_DIM)
    tile_spec = pl.BlockSpec((ATTN_BATCH, ATTN_TILE, LANES), lambda b, p, i: (b, i, p))
    seq_spec = pl.BlockSpec((ATTN_BATCH, s, LANES), lambda b, p, i: (b, 0, p))
    return pl.pallas_call(
        _attn_kernel,
        grid=(bsz // ATTN_BATCH, pairs, s // ATTN_TILE),
        in_specs=[tile_spec, seq_spec, seq_spec],
        out_specs=tile_spec,
        out_shape=jax.ShapeDtypeStruct((bsz, s, d_sb), f32),
        scratch_shapes=[pltpu.VMEM((n_streams, ATTN_TILE, ATTN_TILE), f32),
                        pltpu.VMEM((n_streams, ATTN_TILE, ATTN_TILE), f32),
                        pltpu.VMEM((n_streams, ATTN_TILE, LANES), f32),
                        pltpu.VMEM((n_streams, ATTN_TILE, LANES), f32)],
        compiler_params=pltpu.CompilerParams(
            dimension_semantics=("arbitrary", "arbitrary", "arbitrary"),
            vmem_limit_bytes=VMEM_LIMIT),
        name="attention",
    )(q, k, v)


def _out_proj_kernel(x_ref, yc_ref, ya_ref, az_ref, p_ref, bg_ref, wo_ref, pg_ref, wpg_ref, bpg_ref,
                     wpe_ref, fg_ref, o_ref, *, layer, d_conv, final):
    this_layer = slice(layer, layer + 1)
    lo_half = lax.broadcasted_iota(jnp.int32, (1, LANES), 1) < HEAD_DIM

    def attn_branch(rows):
        blocks = []
        for j in range(ya_ref.shape[1] // LANES):
            sl = slice(j * LANES, (j + 1) * LANES)
            y = _pair_group_norm(ya_ref[rows, sl], lo_half) * bg_ref[this_layer, sl]
            blocks.append((y * _silu(az_ref[rows, sl])).astype(bf16))
        return jnp.concatenate(blocks, axis=1)

    halves = [pl.ds(r * (x_ref.shape[0] // 2), x_ref.shape[0] // 2) for r in range(2)]
    x1 = [x_ref[rows, :]
          + jnp.dot(yc_ref[rows, :], wo_ref[0:d_conv, :], preferred_element_type=f32)
          + jnp.dot(attn_branch(rows), wo_ref[d_conv:, :], preferred_element_type=f32)
          for rows in halves]
    for rows, x1_r in zip(halves, x1):
        h = _rmsnorm(x1_r, pg_ref[this_layer, :])
        gate = _sigmoid(jnp.dot(h, wpg_ref[...], preferred_element_type=f32) + bpg_ref[this_layer, :])
        pe = jnp.dot(p_ref[rows, :], wpe_ref[...], preferred_element_type=f32)
        x2 = x1_r + gate * pe
        o_ref[rows, :] = _rmsnorm(x2, fg_ref[...]) if final else x2


def _out_proj(layer, x, y_c, y_a, a_z, p, branch_g, w_out, ple_g, w_pg, b_pg, w_pe, final_g, final):
    n, d = x.shape
    d_conv, d_sb = y_c.shape[1], y_a.shape[1]
    depth, _, ple = p.shape
    assert d_conv % d_sb == 0
    row_spec = lambda width: pl.BlockSpec((ROW_TILE, width), lambda t: (t, 0))
    const = lambda shape: pl.BlockSpec(shape, lambda t: (0, 0), pipeline_mode=pl.Buffered(1))
    of_layer = lambda shape: pl.BlockSpec((None,) + shape, lambda t: (layer, 0, 0),
                                          pipeline_mode=pl.Buffered(1))
    return pl.pallas_call(
        functools.partial(_out_proj_kernel, layer=layer, d_conv=d_conv, final=final),
        grid=(n // ROW_TILE,),
        in_specs=[row_spec(d), row_spec(d_conv), row_spec(d_sb), row_spec(d_sb),
                  pl.BlockSpec((None, ROW_TILE, ple), lambda t: (layer, t, 0)),
                  pl.BlockSpec((depth, d_sb), lambda t: (0, d_conv // d_sb),
                               pipeline_mode=pl.Buffered(1)),
                  of_layer((d_conv + d_sb, d)), const((depth, d)), of_layer((d, d)), const((depth, d)),
                  of_layer((ple, d)), const((1, d))],
        out_specs=row_spec(d),
        out_shape=jax.ShapeDtypeStruct((n, d), f32),
        compiler_params=pltpu.CompilerParams(
            dimension_semantics=("arbitrary",), vmem_limit_bytes=VMEM_LIMIT),
        name="out_proj",
    )(x, y_c, y_a, a_z, p, branch_g, w_out, ple_g, w_pg, b_pg, w_pe, final_g)


def kernel(x, p, norm_g, w_in, conv_w, conv_b, branch_g, w_out, ple_norm_g, w_pg, b_pg, w_pe, final_g):
    depth = w_in.shape[0]
    bsz, s, d = x.shape
    d_conv = conv_w.shape[2]
    d_sb = branch_g.shape[1] - d_conv
    assert s % ROW_TILE == 0 and s % ATTN_TILE == 0 and d_sb % LANES == 0 and d_conv % LANES == 0
    assert bsz % ATTN_BATCH == 0
    w_out = w_out.astype(bf16)
    p = p.reshape(depth, bsz * s, -1)
    final_g = final_g.reshape(1, -1)
    for i in range(depth):
        y_c, q, k, v, a_z = _in_proj(i, x, norm_g, w_in, conv_w, conv_b, branch_g, d_conv, d_sb)
        y_a = _attention(q, k, v)
        x = _out_proj(
            i, x.reshape(bsz * s, d), y_c.reshape(bsz * s, d_conv), y_a.reshape(bsz * s, d_sb),
            a_z.reshape(bsz * s, d_sb), p, branch_g, w_out, ple_norm_g, w_pg, b_pg, w_pe, final_g,
            final=(i == depth - 1)).reshape(bsz, s, d)
    return x
```

```python
import functools

import jax
import jax.numpy as jnp
from jax import lax
from jax.experimental import pallas as pl
from jax.experimental.pallas import tpu as pltpu

HEAD_DIM = 64
LANES = 128
CONV_WIDTH = 3
EPS = 1e-6

ROW_TILE = 1024
ATTN_TILE = 256
DEAD_CARRY = 128.0
SCORES_AHEAD = 2
ATTN_BATCH = 4
CARRY_ROWS = 8
VMEM_LIMIT = 48 * 1024 * 1024
LOG2_E = 1.4426950408889634

f32 = jnp.float32
bf16 = jnp.bfloat16


def _rmsnorm(x, g):
    return x * lax.rsqrt(jnp.mean(x * x, axis=-1, keepdims=True) + EPS) * g


def _sigmoid(x):
    return 1.0 / (1.0 + jnp.exp(-x))


def _silu(x):
    h = 0.5 * x
    return h + h * jnp.tanh(h)


def _pair_group_norm(y, lo_half):
    sq = y * y
    s_lo = jnp.sum(jnp.where(lo_half, sq, 0.0), axis=-1, keepdims=True)
    s_hi = jnp.sum(jnp.where(lo_half, 0.0, sq), axis=-1, keepdims=True)
    inv_lo = lax.rsqrt(s_lo * (1.0 / HEAD_DIM) + EPS)
    inv_hi = lax.rsqrt(s_hi * (1.0 / HEAD_DIM) + EPS)
    return y * jnp.where(lo_half, inv_lo, inv_hi)


def _in_proj_kernel(x_ref, g_ref, w_ref, cw_ref, cb_ref, bg_ref,
                    yc_ref, q_ref, k_ref, v_ref, az_ref, u_scr, *, layer, d_conv, d_sb):
    rows = x_ref.shape[1]
    this_layer = slice(layer, layer + 1)

    @pl.when(pl.program_id(1) == 0)
    def _():
        u_scr[0:CARRY_ROWS, :] = jnp.zeros((CARRY_ROWS, d_conv), f32)

    h = _rmsnorm(x_ref[0], g_ref[this_layer, :])

    def proj(col0, width):
        return jnp.dot(h, w_ref[:, col0:col0 + width], preferred_element_type=f32)

    c_b = proj(0, d_conv)
    u_scr[CARRY_ROWS:CARRY_ROWS + rows, :] = proj(d_conv, d_conv) * proj(2 * d_conv, d_conv)
    c_z = proj(3 * d_conv, d_conv)
    lo_half = lax.broadcasted_iota(jnp.int32, (1, LANES), 1) < HEAD_DIM
    for j in range(d_conv // LANES):
        sl = slice(j * LANES, (j + 1) * LANES)
        conv = cb_ref[this_layer, sl]
        for tap in range(CONV_WIDTH):
            back = CONV_WIDTH - 1 - tap
            conv = conv + u_scr[CARRY_ROWS - back:CARRY_ROWS - back + rows, sl] * cw_ref[tap:tap + 1, sl]
        y = _pair_group_norm(c_b[:, sl] * conv, lo_half) * bg_ref[this_layer, sl]
        yc_ref[0, :, sl] = (y * _silu(c_z[:, sl])).astype(bf16)
    u_scr[0:CARRY_ROWS, :] = u_scr[rows:rows + CARRY_ROWS, :]

    base = 4 * d_conv
    q_ref[0] = (proj(base, d_sb) * (HEAD_DIM ** -0.5)).astype(bf16)
    k_ref[0] = proj(base + d_sb, d_sb).astype(bf16)
    v_ref[0] = proj(base + 2 * d_sb, d_sb).astype(bf16)
    az_ref[0] = proj(base + 3 * d_sb, d_sb)


def _in_proj(layer, x, norm_g, w_in, conv_w, conv_b, branch_g, d_conv, d_sb):
    bsz, s, d = x.shape
    depth, _, n_in = w_in.shape
    row_spec = lambda width: pl.BlockSpec((1, ROW_TILE, width), lambda b, t: (b, t, 0))
    const = lambda shape: pl.BlockSpec(shape, lambda b, t: (0, 0), pipeline_mode=pl.Buffered(1))
    of_layer = lambda shape: pl.BlockSpec((None,) + shape, lambda b, t: (layer, 0, 0),
                                          pipeline_mode=pl.Buffered(1))
    return pl.pallas_call(
        functools.partial(_in_proj_kernel, layer=layer, d_conv=d_conv, d_sb=d_sb),
        grid=(bsz, s // ROW_TILE),
        in_specs=[row_spec(d), const((depth, d)), of_layer((d, n_in)),
                  of_layer((CONV_WIDTH, d_conv)), const((depth, d_conv)), const((depth, d_conv))],
        out_specs=[row_spec(d_conv), row_spec(d_sb), row_spec(d_sb), row_spec(d_sb), row_spec(d_sb)],
        out_shape=[jax.ShapeDtypeStruct((bsz, s, d_conv), bf16),
                   jax.ShapeDtypeStruct((bsz, s, d_sb), bf16),
                   jax.ShapeDtypeStruct((bsz, s, d_sb), bf16),
                   jax.ShapeDtypeStruct((bsz, s, d_sb), bf16),
                   jax.ShapeDtypeStruct((bsz, s, d_sb), f32)],
        scratch_shapes=[pltpu.VMEM((ROW_TILE + CARRY_ROWS, d_conv), f32)],
        compiler_params=pltpu.CompilerParams(
            dimension_semantics=("arbitrary", "arbitrary"), vmem_limit_bytes=VMEM_LIMIT),
        name="in_proj",
    )(x, norm_g, w_in, conv_w, conv_b, branch_g)


MASKED = -1e30


def _attn_kernel(q_ref, k_ref, v_ref, o_ref, z_scr, d_scr, c_scr, acc_scr):
    t = ATTN_TILE
    i = pl.program_id(2)
    lo_half = lax.broadcasted_iota(jnp.int32, (1, LANES), 1) < HEAD_DIM
    key_pos = lax.broadcasted_iota(jnp.int32, (t, t), 1)
    query_pos = lax.broadcasted_iota(jnp.int32, (t, t), 0)
    causal = key_pos < query_pos
    keys_from = (query_pos >= key_pos).astype(bf16)
    streams = []
    for b in range(q_ref.shape[0]):
        q = q_ref[b]
        zero = jnp.zeros_like(q)
        streams.append((b, jnp.where(lo_half, q, zero)))
        streams.append((b, jnp.where(lo_half, zero, q)))

    def rows_of_tile(m):
        return pl.ds(pl.multiple_of(jnp.maximum(i - m, 0) * t, t), t)

    def scores(n, m):
        b, q_h = streams[n]
        z_scr[n] = lax.dot_general(q_h, k_ref[b, rows_of_tile(m), :], (((1,), (1,)), ((), ())),
                                   preferred_element_type=f32)

    def decay(n, diagonal=False):
        z = z_scr[n]
        c = c_scr[n]
        d_scr[n] = z - jnp.concatenate([c] * (t // LANES), axis=1)
        sp =jnp.maximum(z, 0.0) + jnp.log(1.0 + jnp.exp2(jnp.abs(z) * -LOG2_E))
        if diagonal:
            sp = jnp.where(causal, sp, 0.0)
        suffix = jnp.dot(sp.astype(bf16), keys_from, preferred_element_type=f32)
        d = d_scr[n] - suffix
        if diagonal:
            d = jnp.where(causal, d, MASKED)
        d_scr[n] = d
        c_new = c + suffix[:, 0:1]
        c_scr[n] = c_new
        return c_new

    def weigh(n, m, exists=None):
        a = jnp.exp(d_scr[n]).astype(bf16)
        out = jnp.dot(a, v_ref[streams[n][0], rows_of_tile(m), :], preferred_element_type=f32)
        acc_scr[n] += out if exists is None else jnp.where(exists, out, 0.0)

    def step(carry, look_ahead=True):
        m, _ = carry
        c_low = None
        for n in range(len(streams)):
            weigh(n, m)
            c_new = decay(n)
            c_low = c_new if c_low is None else jnp.minimum(c_low, c_new)
            if look_ahead:
                scores(n, m + 2)
        return m + 1, jnp.min(c_low)

    def more_to_do(carry):
        m, c_low = carry
        return jnp.logical_and(m < i, c_low < DEAD_CARRY)

    c_scr[...] = jnp.zeros(c_scr.shape, f32)
    acc_scr[...] = jnp.zeros(acc_scr.shape, f32)
    for n in range(SCORES_AHEAD):
        scores(n, 0)
    for n in range(len(streams)):
        if n + SCORES_AHEAD < len(streams):
            scores(n + SCORES_AHEAD, 0)
        decay(n, diagonal=True)
        scores(n, 1)
    first = step((jnp.int32(0), None), look_ahead=False)

    @pl.when(more_to_do(first))
    def _():
        for n in range(len(streams)):
            scores(n, 2)

    steps_done, _ = lax.while_loop(more_to_do, step, first)

    for n in range(len(streams)):
        weigh(n, steps_done, exists=steps_done <= i)

    for b in range(q_ref.shape[0]):
        o_ref[b] = jnp.where(lo_half, acc_scr[2 * b], acc_scr[2 * b + 1])


def _attention(q, k, v):
    bsz, s, d_sb = q.shape
    pairs = d_sb // LANES
    n_streams = ATTN_BATCH * (LANES // HEAD_DIM)
    tile_spec = pl.BlockSpec((ATTN_BATCH, ATTN_TILE, LANES), lambda b, p, i: (b, i, p))
    seq_spec = pl.BlockSpec((ATTN_BATCH, s, LANES), lambda b, p, i: (b, 0, p))
    return pl.pallas_call(
        _attn_kernel,
        grid=(bsz // ATTN_BATCH, pairs, s // ATTN_TILE),
        in_specs=[tile_spec, seq_spec, seq_spec],
        out_specs=tile_spec,
        out_shape=jax.ShapeDtypeStruct((bsz, s, d_sb), f32),
        scratch_shapes=[pltpu.VMEM((n_streams, ATTN_TILE, ATTN_TILE), f32),
                        pltpu.VMEM((n_streams, ATTN_TILE, ATTN_TILE), f32),
                        pltpu.VMEM((n_streams, ATTN_TILE, LANES), f32),
                        pltpu.VMEM((n_streams, ATTN_TILE, LANES), f32)],
        compiler_params=pltpu.CompilerParams(
            dimension_semantics=("arbitrary", "arbitrary", "arbitrary"),
            vmem_limit_bytes=VMEM_LIMIT),
        name="attention",
    )(q, k, v)


def _out_proj_kernel(x_ref, yc_ref, ya_ref, az_ref, p_ref, bg_ref, wo_ref, pg_ref, wpg_ref, bpg_ref,
                     wpe_ref, fg_ref, o_ref, *, layer, d_conv, final):
    this_layer = slice(layer, layer + 1)
    lo_half = lax.broadcasted_iota(jnp.int32, (1, LANES), 1) < HEAD_DIM

    def attn_branch(rows):
        blocks = []
        for j in range(ya_ref.shape[1] // LANES):
            sl = slice(j * LANES, (j + 1) * LANES)
            y = _pair_group_norm(ya_ref[rows, sl], lo_half) * bg_ref[this_layer, sl]
            blocks.append((y * _silu(az_ref[rows, sl])).astype(bf16))
        return jnp.concatenate(blocks, axis=1)

    halves = [pl.ds(r * (x_ref.shape[0] // 2), x_ref.shape[0] // 2) for r in range(2)]
    x1 = [x_ref[rows, :]
          + jnp.dot(yc_ref[rows, :], wo_ref[0:d_conv, :], preferred_element_type=f32)
          + jnp.dot(attn_branch(rows), wo_ref[d_conv:, :], preferred_element_type=f32)
          for rows in halves]
    for rows, x1_r in zip(halves, x1):
        h = _rmsnorm(x1_r, pg_ref[this_layer, :])
        gate = _sigmoid(jnp.dot(h, wpg_ref[...], preferred_element_type=f32) + bpg_ref[this_layer, :])
        pe = jnp.dot(p_ref[rows, :], wpe_ref[...], preferred_element_type=f32)
        x2 = x1_r + gate * pe
        o_ref[rows, :] = _rmsnorm(x2, fg_ref[...]) if final else x2


def _out_proj(layer, x, y_c, y_a, a_z, p, branch_g, w_out, ple_g, w_pg, b_pg, w_pe, final_g, final):
    n, d = x.shape
    d_conv, d_sb = y_c.shape[1], y_a.shape[1]
    depth, _, ple = p.shape
    assert d_conv % d_sb == 0
    row_spec = lambda width: pl.BlockSpec((ROW_TILE, width), lambda t: (t, 0))
    const = lambda shape: pl.BlockSpec(shape, lambda t: (0, 0), pipeline_mode=pl.Buffered(1))
    of_layer = lambda shape: pl.BlockSpec((None,) + shape, lambda t: (layer, 0, 0),
                                          pipeline_mode=pl.Buffered(1))
    return pl.pallas_call(
        functools.partial(_out_proj_kernel, layer=layer, d_conv=d_conv, final=final),
        grid=(n // ROW_TILE,),
        in_specs=[row_spec(d), row_spec(d_conv), row_spec(d_sb), row_spec(d_sb),
                  pl.BlockSpec((None, ROW_TILE, ple), lambda t: (layer, t, 0)),
                  pl.BlockSpec((depth, d_sb), lambda t: (0, d_conv // d_sb),
                               pipeline_mode=pl.Buffered(1)),
                  of_layer((d_conv + d_sb, d)), const((depth, d)), of_layer((d, d)), const((depth, d)),
                  of_layer((ple, d)), const((1, d))],
        out_specs=row_spec(d),
        out_shape=jax.ShapeDtypeStruct((n, d), f32),
        compiler_params=pltpu.CompilerParams(
            dimension_semantics=("arbitrary",), vmem_limit_bytes=VMEM_LIMIT),
        name="out_proj",
    )(x, y_c, y_a, a_z, p, branch_g, w_out, ple_g, w_pg, b_pg, w_pe, final_g)


def kernel(x, p, norm_g, w_in, conv_w, conv_b, branch_g, w_out, ple_norm_g, w_pg, b_pg, w_pe, final_g):
    depth = w_in.shape[0]
    bsz, s, d = x.shape
    d_conv = conv_w.shape[2]
    d_sb = branch_g.shape[1] - d_conv
    assert s % ROW_TILE == 0 and s % ATTN_TILE == 0 and d_sb % LANES == 0 and d_conv % LANES == 0
    assert bsz % ATTN_BATCH == 0
    w_out = w_out.astype(bf16)
    p = p.reshape(depth, bsz * s, -1)
    final_g = final_g.reshape(1, -1)
    for i in range(depth):
        y_c, q, k, v, a_z = _in_proj(i, x, norm_g, w_in, conv_w, conv_b, branch_g, d_conv, d_sb)
        y_a = _attention(q, k, v)
        x = _out_proj(
            i, x.reshape(bsz * s, d), y_c.reshape(bsz * s, d_conv), y_a.reshape(bsz * s, d_sb),
            a_z.reshape(bsz * s, d_sb), p, branch_g, w_out, ple_norm_g, w_pg, b_pg, w_pe, final_g,
            final=(i == depth - 1)).reshape(bsz, s, d)
    return x
```

```python
import functools

import jax
import jax.numpy as jnp
from jax import lax
from jax.experimental import pallas as pl
from jax.experimental.pallas import tpu as pltpu

HEAD_DIM = 64
LANES = 128
CONV_WIDTH = 3
EPS = 1e-6

ROW_TILE = 1024
ATTN_TILE = 256
DEAD_CARRY = 128.0
SCORES_AHEAD = 2
ATTN_BATCH = 4
CARRY_ROWS = 8
VMEM_LIMIT = 48 * 1024 * 1024
LOG2_E = 1.4426950408889634

f32 = jnp.float32
bf16 = jnp.bfloat16


def _rmsnorm(x, g):
    return x * lax.rsqrt(jnp.mean(x * x, axis=-1, keepdims=True) + EPS) * g


def _sigmoid(x):
    return 1.0 / (1.0 + jnp.exp(-x))


def _silu(x):
    h = 0.5 * x
    return h + h * jnp.tanh(h)


def _pair_group_norm(y, lo_half):
    sq = y * y
    s_lo = jnp.sum(jnp.where(lo_half, sq, 0.0), axis=-1, keepdims=True)
    s_hi = jnp.sum(jnp.where(lo_half, 0.0, sq), axis=-1, keepdims=True)
    inv_lo = lax.rsqrt(s_lo * (1.0 / HEAD_DIM) + EPS)
    inv_hi = lax.rsqrt(s_hi * (1.0 / HEAD_DIM) + EPS)
    return y * jnp.where(lo_half, inv_lo, inv_hi)


def _in_proj_kernel(x_ref, g_ref, w_ref, cw_ref, cb_ref, bg_ref,
                    yc_ref, q_ref, k_ref, v_ref, az_ref, u_scr, *, layer, d_conv, d_sb):
    rows = x_ref.shape[1]
    this_layer = slice(layer, layer + 1)

    @pl.when(pl.program_id(1) == 0)
    def _():
        u_scr[0:CARRY_ROWS, :] = jnp.zeros((CARRY_ROWS, d_conv), f32)

    h = _rmsnorm(x_ref[0], g_ref[this_layer, :])

    def proj(col0, width):
        return jnp.dot(h, w_ref[:, col0:col0 + width], preferred_element_type=f32)

    c_b = proj(0, d_conv)
    u_scr[CARRY_ROWS:CARRY_ROWS + rows, :] = proj(d_conv, d_conv) * proj(2 * d_conv, d_conv)
    c_z = proj(3 * d_conv, d_conv)
    lo_half = lax.broadcasted_iota(jnp.int32, (1, LANES), 1) < HEAD_DIM
    for j in range(d_conv // LANES):
        sl = slice(j * LANES, (j + 1) * LANES)
        conv = cb_ref[this_layer, sl]
        for tap in range(CONV_WIDTH):
            back = CONV_WIDTH - 1 - tap
            conv = conv + u_scr[CARRY_ROWS - back:CARRY_ROWS - back + rows, sl] * cw_ref[tap:tap + 1, sl]
        y = _pair_group_norm(c_b[:, sl] * conv, lo_half) * bg_ref[this_layer, sl]
        yc_ref[0, :, sl] = (y * _silu(c_z[:, sl])).astype(bf16)
    u_scr[0:CARRY_ROWS, :] = u_scr[rows:rows + CARRY_ROWS, :]

    base = 4 * d_conv
    q_ref[0] = (proj(base, d_sb) * (HEAD_DIM ** -0.5)).astype(bf16)
    k_ref[0] = proj(base + d_sb, d_sb).astype(bf16)
    v_ref[0] = proj(base + 2 * d_sb, d_sb).astype(bf16)
    az_ref[0] = proj(base + 3 * d_sb, d_sb)


def _in_proj(layer, x, norm_g, w_in, conv_w, conv_b, branch_g, d_conv, d_sb):
    bsz, s, d = x.shape
    depth, _, n_in = w_in.shape
    row_spec = lambda width: pl.BlockSpec((1, ROW_TILE, width), lambda b, t: (b, t, 0))
    const = lambda shape: pl.BlockSpec(shape, lambda b, t: (0, 0), pipeline_mode=pl.Buffered(1))
    of_layer = lambda shape: pl.BlockSpec((None,) + shape, lambda b, t: (layer, 0, 0),
                                          pipeline_mode=pl.Buffered(1))
    return pl.pallas_call(
        functools.partial(_in_proj_kernel, layer=layer, d_conv=d_conv, d_sb=d_sb),
        grid=(bsz, s // ROW_TILE),
        in_specs=[row_spec(d), const((depth, d)), of_layer((d, n_in)),
                  of_layer((CONV_WIDTH, d_conv)), const((depth, d_conv)), const((depth, d_conv))],
        out_specs=[row_spec(d_conv), row_spec(d_sb), row_spec(d_sb), row_spec(d_sb), row_spec(d_sb)],
        out_shape=[jax.ShapeDtypeStruct((bsz, s, d_conv), bf16),
                   jax.ShapeDtypeStruct((bsz, s, d_sb), bf16),
                   jax.ShapeDtypeStruct((bsz, s, d_sb), bf16),
                   jax.ShapeDtypeStruct((bsz, s, d_sb), bf16),
                   jax.ShapeDtypeStruct((bsz, s, d_sb), f32)],
        scratch_shapes=[pltpu.VMEM((ROW_TILE + CARRY_ROWS, d_conv), f32)],
        compiler_params=pltpu.CompilerParams(
            dimension_semantics=("arbitrary", "arbitrary"), vmem_limit_bytes=VMEM_LIMIT),
        name="in_proj",
    )(x, norm_g, w_in, conv_w, conv_b, branch_g)


MASKED = -1e30


def _attn_kernel(q_ref, k_ref, v_ref, o_ref, z_scr, d_scr, c_scr, acc_scr):
    t = ATTN_TILE
    i = pl.program_id(2)
    lo_half = lax.broadcasted_iota(jnp.int32, (1, LANES), 1) < HEAD_DIM
    key_pos = lax.broadcasted_iota(jnp.int32, (t, t), 1)
    query_pos = lax.broadcasted_iota(jnp.int32, (t, t), 0)
    causal = key_pos < query_pos
    keys_from = (query_pos >= key_pos).astype(bf16)
    streams = []
    for b in range(q_ref.shape[0]):
        q = q_ref[b]
        zero = jnp.zeros_like(q)
        streams.append((b, jnp.where(lo_half, q, zero)))
        streams.append((b, jnp.where(lo_half, zero, q)))

    def rows_of_tile(m):
        return pl.ds(pl.multiple_of(jnp.maximum(i - m, 0) * t, t), t)

    def scores(n, m):
        b, q_h = streams[n]
        z_scr[n] = lax.dot_general(q_h, k_ref[b, rows_of_tile(m), :], (((1,), (1,)), ((), ())),
                                   preferred_element_type=f32)

    def decay(n, diagonal=False):
        z = z_scr[n]
        c = c_scr[n]
        d_scr[n] = z - jnp.concatenate([c] * (t // LANES), axis=1)
        sp =jnp.maximum(z, 0.0) + jnp.log(1.0 + jnp.exp2(jnp.abs(z) * -LOG2_E))
        if diagonal:
            sp = jnp.where(causal, sp, 0.0)
        suffix = jnp.dot(sp.astype(bf16), keys_from, preferred_element_type=f32)
        d = d_scr[n] - suffix
        if diagonal:
            d = jnp.where(causal, d, MASKED)
        d_scr[n] = d
        c_new = c + suffix[:, 0:1]
        c_scr[n] = c_new
        return c_new

    def weigh(n, m, exists=None):
        a = jnp.exp(d_scr[n]).astype(bf16)
        out = jnp.dot(a, v_ref[streams[n][0], rows_of_tile(m), :], preferred_element_type=f32)
        acc_scr[n] += out if exists is None else jnp.where(exists, out, 0.0)

    def lowest(carries):
        return jnp.min(functools.reduce(jnp.minimum, carries))

    def step(carry):
        m, _ = carry
        c_new = []
        for n in range(len(streams)):
            weigh(n, m)
            c_new.append(decay(n))
            scores(n, m + 2)
        return m + 1, lowest(c_new)

    def more_to_do(carry):
        m, c_low = carry
        return jnp.logical_and(m < i, c_low < DEAD_CARRY)

    c_scr[...] = jnp.zeros(c_scr.shape, f32)
    acc_scr[...] = jnp.zeros(acc_scr.shape, f32)
    for n in range(SCORES_AHEAD):
        scores(n, 0)
    for n in range(len(streams)):
        if n + SCORES_AHEAD < len(streams):
            scores(n + SCORES_AHEAD, 0)
        decay(n, diagonal=True)
        scores(n, 1)
    c_new = []
    for n in range(len(streams)):
        weigh(n, 0)
        c_new.append(decay(n))
    for n in range(len(streams)):
        weigh(n, 1, exists=i >= 1)

    @pl.when(more_to_do((1, lowest(c_new))))
    def _():
        for n in range(len(streams)):
            scores(n, 2)
        c_new = [decay(n) for n in range(len(streams))]
        for n in range(len(streams)):
            scores(n, 3)
        last, _ = lax.while_loop(more_to_do, step, (jnp.int32(2), lowest(c_new)))
        for n in range(len(streams)):
            weigh(n, last)

    for b in range(q_ref.shape[0]):
        o_ref[b] = jnp.where(lo_half, acc_scr[2 * b], acc_scr[2 * b + 1])


def _attention(q, k, v):
    bsz, s, d_sb = q.shape
    pairs = d_sb // LANES
    n_streams = ATTN_BATCH * (LANES // HEAD_DIM)
    tile_spec = pl.BlockSpec((ATTN_BATCH, ATTN_TILE, LANES), lambda b, p, i: (b, i, p))
    seq_spec = pl.BlockSpec((ATTN_BATCH, s, LANES), lambda b, p, i: (b, 0, p))
    return pl.pallas_call(
        _attn_kernel,
        grid=(bsz // ATTN_BATCH, pairs, s // ATTN_TILE),
        in_specs=[tile_spec, seq_spec, seq_spec],
        out_specs=tile_spec,
        out_shape=jax.ShapeDtypeStruct((bsz, s, d_sb), f32),
        scratch_shapes=[pltpu.VMEM((n_streams, ATTN_TILE, ATTN_TILE), f32),
                        pltpu.VMEM((n_streams, ATTN_TILE, ATTN_TILE), f32),
                        pltpu.VMEM((n_streams, ATTN_TILE, LANES), f32),
                        pltpu.VMEM((n_streams, ATTN_TILE, LANES), f32)],
        compiler_params=pltpu.CompilerParams(
            dimension_semantics=("arbitrary", "arbitrary", "arbitrary"),
            vmem_limit_bytes=VMEM_LIMIT),
        name="attention",
    )(q, k, v)


def _out_proj_kernel(x_ref, yc_ref, ya_ref, az_ref, p_ref, bg_ref, wo_ref, pg_ref, wpg_ref, bpg_ref,
                     wpe_ref, fg_ref, o_ref, *, layer, d_conv, final):
    this_layer = slice(layer, layer + 1)
    lo_half = lax.broadcasted_iota(jnp.int32, (1, LANES), 1) < HEAD_DIM

    def attn_branch(rows):
        blocks = []
        for j in range(ya_ref.shape[1] // LANES):
            sl = slice(j * LANES, (j + 1) * LANES)
            y = _pair_group_norm(ya_ref[rows, sl], lo_half) * bg_ref[this_layer, sl]
            blocks.append((y * _silu(az_ref[rows, sl])).astype(bf16))
        return jnp.concatenate(blocks, axis=1)

    halves = [pl.ds(r * (x_ref.shape[0] // 2), x_ref.shape[0] // 2) for r in range(2)]
    x1 = [x_ref[rows, :]
          + jnp.dot(yc_ref[rows, :], wo_ref[0:d_conv, :], preferred_element_type=f32)
          + jnp.dot(attn_branch(rows), wo_ref[d_conv:, :], preferred_element_type=f32)
          for rows in halves]
    for rows, x1_r in zip(halves, x1):
        h = _rmsnorm(x1_r, pg_ref[this_layer, :])
        gate = _sigmoid(jnp.dot(h, wpg_ref[...], preferred_element_type=f32) + bpg_ref[this_layer, :])
        pe = jnp.dot(p_ref[rows, :], wpe_ref[...], preferred_element_type=f32)
        x2 = x1_r + gate * pe
        o_ref[rows, :] = _rmsnorm(x2, fg_ref[...]) if final else x2


def _out_proj(layer, x, y_c, y_a, a_z, p, branch_g, w_out, ple_g, w_pg, b_pg, w_pe, final_g, final):
    n, d = x.shape
    d_conv, d_sb = y_c.shape[1], y_a.shape[1]
    depth, _, ple = p.shape
    assert d_conv % d_sb == 0
    row_spec = lambda width: pl.BlockSpec((ROW_TILE, width), lambda t: (t, 0))
    const = lambda shape: pl.BlockSpec(shape, lambda t: (0, 0), pipeline_mode=pl.Buffered(1))
    of_layer = lambda shape: pl.BlockSpec((None,) + shape, lambda t: (layer, 0, 0),
                                          pipeline_mode=pl.Buffered(1))
    return pl.pallas_call(
        functools.partial(_out_proj_kernel, layer=layer, d_conv=d_conv, final=final),
        grid=(n // ROW_TILE,),
        in_specs=[row_spec(d), row_spec(d_conv), row_spec(d_sb), row_spec(d_sb),
                  pl.BlockSpec((None, ROW_TILE, ple), lambda t: (layer, t, 0)),
                  pl.BlockSpec((depth, d_sb), lambda t: (0, d_conv // d_sb),
                               pipeline_mode=pl.Buffered(1)),
                  of_layer((d_conv + d_sb, d)), const((depth, d)), of_layer((d, d)), const((depth, d)),
                  of_layer((ple, d)), const((1, d))],
        out_specs=row_spec(d),
        out_shape=jax.ShapeDtypeStruct((n, d), f32),
        compiler_params=pltpu.CompilerParams(
            dimension_semantics=("arbitrary",), vmem_limit_bytes=VMEM_LIMIT),
        name="out_proj",
    )(x, y_c, y_a, a_z, p, branch_g, w_out, ple_g, w_pg, b_pg, w_pe, final_g)


def kernel(x, p, norm_g, w_in, conv_w, conv_b, branch_g, w_out, ple_norm_g, w_pg, b_pg, w_pe, final_g):
    depth = w_in.shape[0]
    bsz, s, d = x.shape
    d_conv = conv_w.shape[2]
    d_sb = branch_g.shape[1] - d_conv
    assert s % ROW_TILE == 0 and s % ATTN_TILE == 0 and d_sb % LANES == 0 and d_conv % LANES == 0
    assert bsz % ATTN_BATCH == 0
    w_out = w_out.astype(bf16)
    p = p.reshape(depth, bsz * s, -1)
    final_g = final_g.reshape(1, -1)
    for i in range(depth):
        y_c, q, k, v, a_z = _in_proj(i, x, norm_g, w_in, conv_w, conv_b, branch_g, d_conv, d_sb)
        y_a = _attention(q, k, v)
        x = _out_proj(
            i, x.reshape(bsz * s, d), y_c.reshape(bsz * s, d_conv), y_a.reshape(bsz * s, d_sb),
            a_z.reshape(bsz * s, d_sb), p, branch_g, w_out, ple_norm_g, w_pg, b_pg, w_pe, final_g,
            final=(i == depth - 1)).reshape(bsz, s, d)
    return x
```

```python
import functools

import jax
import jax.numpy as jnp
from jax import lax
from jax.experimental import pallas as pl
from jax.experimental.pallas import tpu as pltpu

HEAD_DIM = 64
LANES = 128
CONV_WIDTH = 3
EPS = 1e-6

ROW_TILE = 1024
ATTN_TILE = 256
DEAD_CARRY = 128.0
SCORES_AHEAD = 2
ATTN_BATCH = 4
CARRY_ROWS = 8
VMEM_LIMIT = 48 * 1024 * 1024
LOG2_E = 1.4426950408889634

f32 = jnp.float32
bf16 = jnp.bfloat16


def _rmsnorm(x, g):
    return x * lax.rsqrt(jnp.mean(x * x, axis=-1, keepdims=True) + EPS) * g


def _sigmoid(x):
    return 1.0 / (1.0 + jnp.exp(-x))


def _silu(x):
    h = 0.5 * x
    return h + h * jnp.tanh(h)


def _pair_group_norm(y, lo_half):
    sq = y * y
    s_lo = jnp.sum(jnp.where(lo_half, sq, 0.0), axis=-1, keepdims=True)
    s_hi = jnp.sum(jnp.where(lo_half, 0.0, sq), axis=-1, keepdims=True)
    inv_lo = lax.rsqrt(s_lo * (1.0 / HEAD_DIM) + EPS)
    inv_hi = lax.rsqrt(s_hi * (1.0 / HEAD_DIM) + EPS)
    return y * jnp.where(lo_half, inv_lo, inv_hi)


def _in_proj_kernel(x_ref, g_ref, w_ref, cw_ref, cb_ref, bg_ref,
                    yc_ref, q_ref, k_ref, v_ref, az_ref, u_scr, *, layer, d_conv, d_sb):
    rows = x_ref.shape[1]
    this_layer = slice(layer, layer + 1)

    @pl.when(pl.program_id(1) == 0)
    def _():
        u_scr[0:CARRY_ROWS, :] = jnp.zeros((CARRY_ROWS, d_conv), f32)

    h = _rmsnorm(x_ref[0], g_ref[this_layer, :])

    def proj(col0, width):
        return jnp.dot(h, w_ref[:, col0:col0 + width], preferred_element_type=f32)

    c_b = proj(0, d_conv)
    u_scr[CARRY_ROWS:CARRY_ROWS + rows, :] = proj(d_conv, d_conv) * proj(2 * d_conv, d_conv)
    c_z = proj(3 * d_conv, d_conv)
    lo_half = lax.broadcasted_iota(jnp.int32, (1, LANES), 1) < HEAD_DIM
    for j in range(d_conv // LANES):
        sl = slice(j * LANES, (j + 1) * LANES)
        conv = cb_ref[this_layer, sl]
        window = u_scr[:, sl]
        for tap in range(CONV_WIDTH):
            back = CONV_WIDTH - 1 - tap
            taps = pltpu.roll(window, back, axis=0) if back else window
            conv = conv + taps[CARRY_ROWS:, :] * cw_ref[tap:tap + 1, sl]
        y = _pair_group_norm(c_b[:, sl] * conv, lo_half) * bg_ref[this_layer, sl]
        yc_ref[0, :, sl] = (y * _silu(c_z[:, sl])).astype(bf16)
    u_scr[0:CARRY_ROWS, :] = u_scr[rows:rows + CARRY_ROWS, :]

    base = 4 * d_conv
    q_ref[0] = (proj(base, d_sb) * (HEAD_DIM ** -0.5)).astype(bf16)
    k_ref[0] = proj(base + d_sb, d_sb).astype(bf16)
    v_ref[0] = proj(base + 2 * d_sb, d_sb).astype(bf16)
    az_ref[0] = proj(base + 3 * d_sb, d_sb)


def _in_proj(layer, x, norm_g, w_in, conv_w, conv_b, branch_g, d_conv, d_sb):
    bsz, s, d = x.shape
    depth, _, n_in = w_in.shape
    row_spec = lambda width: pl.BlockSpec((1, ROW_TILE, width), lambda b, t: (b, t, 0))
    const = lambda shape: pl.BlockSpec(shape, lambda b, t: (0, 0), pipeline_mode=pl.Buffered(1))
    of_layer = lambda shape: pl.BlockSpec((None,) + shape, lambda b, t: (layer, 0, 0),
                                          pipeline_mode=pl.Buffered(1))
    return pl.pallas_call(
        functools.partial(_in_proj_kernel, layer=layer, d_conv=d_conv, d_sb=d_sb),
        grid=(bsz, s // ROW_TILE),
        in_specs=[row_spec(d), const((depth, d)), of_layer((d, n_in)),
                  of_layer((CONV_WIDTH, d_conv)), const((depth, d_conv)), const((depth, d_conv))],
        out_specs=[row_spec(d_conv), row_spec(d_sb), row_spec(d_sb), row_spec(d_sb), row_spec(d_sb)],
        out_shape=[jax.ShapeDtypeStruct((bsz, s, d_conv), bf16),
                   jax.ShapeDtypeStruct((bsz, s, d_sb), bf16),
                   jax.ShapeDtypeStruct((bsz, s, d_sb), bf16),
                   jax.ShapeDtypeStruct((bsz, s, d_sb), bf16),
                   jax.ShapeDtypeStruct((bsz, s, d_sb), f32)],
        scratch_shapes=[pltpu.VMEM((ROW_TILE + CARRY_ROWS, d_conv), f32)],
        compiler_params=pltpu.CompilerParams(
            dimension_semantics=("arbitrary", "arbitrary"), vmem_limit_bytes=VMEM_LIMIT),
        name="in_proj",
    )(x, norm_g, w_in, conv_w, conv_b, branch_g)


MASKED = -1e30


def _attn_kernel(q_ref, k_ref, v_ref, o_ref, z_scr, d_scr, c_scr, acc_scr):
    t = ATTN_TILE
    i = pl.program_id(2)
    lo_half = lax.broadcasted_iota(jnp.int32, (1, LANES), 1) < HEAD_DIM
    key_pos = lax.broadcasted_iota(jnp.int32, (t, t), 1)
    query_pos = lax.broadcasted_iota(jnp.int32, (t, t), 0)
    causal = key_pos < query_pos
    keys_from = (query_pos >= key_pos).astype(bf16)
    streams = []
    for b in range(q_ref.shape[0]):
        q = q_ref[b]
        zero = jnp.zeros_like(q)
        streams.append((b, jnp.where(lo_half, q, zero)))
        streams.append((b, jnp.where(lo_half, zero, q)))

    def rows_of_tile(m):
        return pl.ds(pl.multiple_of(jnp.maximum(i - m, 0) * t, t), t)

    def scores(n, m):
        b, q_h = streams[n]
        z_scr[n] = lax.dot_general(q_h, k_ref[b, rows_of_tile(m), :], (((1,), (1,)), ((), ())),
                                   preferred_element_type=f32)

    def decay(n, diagonal=False):
        z = z_scr[n]
        c = c_scr[n]
        d_scr[n] = z - jnp.concatenate([c] * (t // LANES), axis=1)
        sp = jnp.maximum(z, 0.0) + jnp.log(1.0 + jnp.exp2(jnp.abs(z) * -LOG2_E))
        if diagonal:
            sp = jnp.where(causal, sp, 0.0)
        suffix = jnp.dot(sp.astype(bf16), keys_from, preferred_element_type=f32)
        d = d_scr[n] - suffix
        if diagonal:
            d = jnp.where(causal, d, MASKED)
        d_scr[n] = d
        c_new = c + suffix[:, 0:1]
        c_scr[n] = c_new
        return c_new

    def weigh(n, m, exists=None):
        a = jnp.exp(d_scr[n]).astype(bf16)
        out = jnp.dot(a, v_ref[streams[n][0], rows_of_tile(m), :], preferred_element_type=f32)
        acc_scr[n] += out if exists is None else jnp.where(exists, out, 0.0)

    def lowest(carries):
        return jnp.min(functools.reduce(jnp.minimum, carries))

    def step(carry):
        m, _ = carry
        c_new = []
        for n in range(len(streams)):
            weigh(n, m)
            c_new.append(decay(n))
            scores(n, m + 2)
        return m + 1, lowest(c_new)

    def more_to_do(carry):
        m, c_low = carry
        return jnp.logical_and(m < i, c_low < DEAD_CARRY)

    c_scr[...] = jnp.zeros(c_scr.shape, f32)
    acc_scr[...] = jnp.zeros(acc_scr.shape, f32)
    for n in range(SCORES_AHEAD):
        scores(n, 0)
    for n in range(len(streams)):
        if n + SCORES_AHEAD < len(streams):
            scores(n + SCORES_AHEAD, 0)
        decay(n, diagonal=True)
        scores(n, 1)
    c_new = []
    for n in range(len(streams)):
        weigh(n, 0)
        c_new.append(decay(n))
    for n in range(len(streams)):
        weigh(n, 1, exists=i >= 1)

    @pl.when(more_to_do((1, lowest(c_new))))
    def _():
        for n in range(len(streams)):
            scores(n, 2)
        c_new = [decay(n) for n in range(len(streams))]
        for n in range(len(streams)):
            scores(n, 3)
        last, _ = lax.while_loop(more_to_do, step, (jnp.int32(2), lowest(c_new)))
        for n in range(len(streams)):
            weigh(n, last)

    for b in range(q_ref.shape[0]):
        o_ref[b] = jnp.where(lo_half, acc_scr[2 * b], acc_scr[2 * b + 1])


def _attention(q, k, v):
    bsz, s, d_sb = q.shape
    pairs = d_sb // LANES
    n_streams = ATTN_BATCH * (LANES // HEAD_DIM)
    tile_spec = pl.BlockSpec((ATTN_BATCH, ATTN_TILE, LANES), lambda b, p, i: (b, i, p))
    seq_spec = pl.BlockSpec((ATTN_BATCH, s, LANES), lambda b, p, i: (b, 0, p))
    return pl.pallas_call(
        _attn_kernel,
        grid=(bsz // ATTN_BATCH, pairs, s // ATTN_TILE),
        in_specs=[tile_spec, seq_spec, seq_spec],
        out_specs=tile_spec,
        out_shape=jax.ShapeDtypeStruct((bsz, s, d_sb), f32),
        scratch_shapes=[pltpu.VMEM((n_streams, ATTN_TILE, ATTN_TILE), f32),
                        pltpu.VMEM((n_streams, ATTN_TILE, ATTN_TILE), f32),
                        pltpu.VMEM((n_streams, ATTN_TILE, LANES), f32),
                        pltpu.VMEM((n_streams, ATTN_TILE, LANES), f32)],
        compiler_params=pltpu.CompilerParams(
            dimension_semantics=("arbitrary", "arbitrary", "arbitrary"),
            vmem_limit_bytes=VMEM_LIMIT),
        name="attention",
    )(q, k, v)


def _out_proj_kernel(x_ref, yc_ref, ya_ref, az_ref, p_ref, bg_ref, wo_ref, pg_ref, wpg_ref, bpg_ref,
                     wpe_ref, fg_ref, o_ref, *, layer, d_conv, final):
    this_layer = slice(layer, layer + 1)
    lo_half = lax.broadcasted_iota(jnp.int32, (1, LANES), 1) < HEAD_DIM

    def attn_branch(rows):
        blocks = []
        for j in range(ya_ref.shape[1] // LANES):
            sl = slice(j * LANES, (j + 1) * LANES)
            y = _pair_group_norm(ya_ref[rows, sl], lo_half) * bg_ref[this_layer, sl]
            blocks.append((y * _silu(az_ref[rows, sl])).astype(bf16))
        return jnp.concatenate(blocks, axis=1)

    halves = [pl.ds(r * (x_ref.shape[0] // 2), x_ref.shape[0] // 2) for r in range(2)]
    x1 = [x_ref[rows, :]
          + jnp.dot(yc_ref[rows, :], wo_ref[0:d_conv, :], preferred_element_type=f32)
          + jnp.dot(attn_branch(rows), wo_ref[d_conv:, :], preferred_element_type=f32)
          for rows in halves]
    for rows, x1_r in zip(halves, x1):
        h = _rmsnorm(x1_r, pg_ref[this_layer, :])
        gate = _sigmoid(jnp.dot(h, wpg_ref[...], preferred_element_type=f32) + bpg_ref[this_layer, :])
        pe = jnp.dot(p_ref[rows, :], wpe_ref[...], preferred_element_type=f32)
        x2 = x1_r + gate * pe
        o_ref[rows, :] = _rmsnorm(x2, fg_ref[...]) if final else x2


def _out_proj(layer, x, y_c, y_a, a_z, p, branch_g, w_out, ple_g, w_pg, b_pg, w_pe, final_g, final):
    n, d = x.shape
    d_conv, d_sb = y_c.shape[1], y_a.shape[1]
    depth, _, ple = p.shape
    assert d_conv % d_sb == 0
    row_spec = lambda width: pl.BlockSpec((ROW_TILE, width), lambda t: (t, 0))
    const = lambda shape: pl.BlockSpec(shape, lambda t: (0, 0), pipeline_mode=pl.Buffered(1))
    of_layer = lambda shape: pl.BlockSpec((None,) + shape, lambda t: (layer, 0, 0),
                                          pipeline_mode=pl.Buffered(1))
    return pl.pallas_call(
        functools.partial(_out_proj_kernel, layer=layer, d_conv=d_conv, final=final),
        grid=(n // ROW_TILE,),
        in_specs=[row_spec(d), row_spec(d_conv), row_spec(d_sb), row_spec(d_sb),
                  pl.BlockSpec((None, ROW_TILE, ple), lambda t: (layer, t, 0)),
                  pl.BlockSpec((depth, d_sb), lambda t: (0, d_conv // d_sb),
                               pipeline_mode=pl.Buffered(1)),
                  of_layer((d_conv + d_sb, d)), const((depth, d)), of_layer((d, d)), const((depth, d)),
                  of_layer((ple, d)), const((1, d))],
        out_specs=row_spec(d),
        out_shape=jax.ShapeDtypeStruct((n, d), f32),
        compiler_params=pltpu.CompilerParams(
            dimension_semantics=("arbitrary",), vmem_limit_bytes=VMEM_LIMIT),
        name="out_proj",
    )(x, y_c, y_a, a_z, p, branch_g, w_out, ple_g, w_pg, b_pg, w_pe, final_g)


def kernel(x, p, norm_g, w_in, conv_w, conv_b, branch_g, w_out, ple_norm_g, w_pg, b_pg, w_pe, final_g):
    depth = w_in.shape[0]
    bsz, s, d = x.shape
    d_conv = conv_w.shape[2]
    d_sb = branch_g.shape[1] - d_conv
    assert s % ROW_TILE == 0 and s % ATTN_TILE == 0 and d_sb % LANES == 0 and d_conv % LANES == 0
    assert bsz % ATTN_BATCH == 0
    w_out = w_out.astype(bf16)
    p = p.reshape(depth, bsz * s, -1)
    final_g = final_g.reshape(1, -1)
    for i in range(depth):
        y_c, q, k, v, a_z = _in_proj(i, x, norm_g, w_in, conv_w, conv_b, branch_g, d_conv, d_sb)
        y_a = _attention(q, k, v)
        x = _out_proj(
            i, x.reshape(bsz * s, d), y_c.reshape(bsz * s, d_conv), y_a.reshape(bsz * s, d_sb),
            a_z.reshape(bsz * s, d_sb), p, branch_g, w_out, ple_norm_g, w_pg, b_pg, w_pe, final_g,
            final=(i == depth - 1)).reshape(bsz, s, d)
    return x
```

```python
import functools

import jax
import jax.numpy as jnp
from jax import lax
from jax.experimental import pallas as pl
from jax.experimental.pallas import tpu as pltpu

HEAD_DIM = 64
LANES = 128
CONV_WIDTH = 3
EPS = 1e-6

ROW_TILE = 1024
ATTN_TILE = 256
DEAD_CARRY = 128.0
SCORES_AHEAD = 3
ATTN_BATCH = 4
CARRY_ROWS = 8
VMEM_LIMIT = 48 * 1024 * 1024
LOG2_E = 1.4426950408889634

f32 = jnp.float32
bf16 = jnp.bfloat16


def _rmsnorm(x, g):
    return x * lax.rsqrt(jnp.mean(x * x, axis=-1, keepdims=True) + EPS) * g


def _sigmoid(x):
    return 1.0 / (1.0 + jnp.exp(-x))


def _silu(x):
    h = 0.5 * x
    return h + h * jnp.tanh(h)


def _pair_group_norm(y, lo_half):
    sq = y * y
    s_lo = jnp.sum(jnp.where(lo_half, sq, 0.0), axis=-1, keepdims=True)
    s_hi = jnp.sum(jnp.where(lo_half, 0.0, sq), axis=-1, keepdims=True)
    inv_lo = lax.rsqrt(s_lo * (1.0 / HEAD_DIM) + EPS)
    inv_hi = lax.rsqrt(s_hi * (1.0 / HEAD_DIM) + EPS)
    return y * jnp.where(lo_half, inv_lo, inv_hi)


def _in_proj_kernel(x_ref, g_ref, w_ref, cw_ref, cb_ref, bg_ref,
                    yc_ref, q_ref, k_ref, v_ref, az_ref, u_scr, *, layer, d_conv, d_sb):
    rows = x_ref.shape[1]
    this_layer = slice(layer, layer + 1)

    @pl.when(pl.program_id(1) == 0)
    def _():
        u_scr[0:CARRY_ROWS, :] = jnp.zeros((CARRY_ROWS, d_conv), f32)

    h = _rmsnorm(x_ref[0], g_ref[this_layer, :])

    def proj(col0, width):
        return jnp.dot(h, w_ref[:, col0:col0 + width], preferred_element_type=f32)

    c_b = proj(0, d_conv)
    u_scr[CARRY_ROWS:CARRY_ROWS + rows, :] = proj(d_conv, d_conv) * proj(2 * d_conv, d_conv)
    c_z = proj(3 * d_conv, d_conv)
    lo_half = lax.broadcasted_iota(jnp.int32, (1, LANES), 1) < HEAD_DIM
    for j in range(d_conv // LANES):
        sl = slice(j * LANES, (j + 1) * LANES)
        conv = cb_ref[this_layer, sl]
        window = u_scr[:, sl]
        for tap in range(CONV_WIDTH):
            back = CONV_WIDTH - 1 - tap
            taps = pltpu.roll(window, back, axis=0) if back else window
            conv = conv + taps[CARRY_ROWS:, :] * cw_ref[tap:tap + 1, sl]
        y = _pair_group_norm(c_b[:, sl] * conv, lo_half) * bg_ref[this_layer, sl]
        yc_ref[0, :, sl] = (y * _silu(c_z[:, sl])).astype(bf16)
    u_scr[0:CARRY_ROWS, :] = u_scr[rows:rows + CARRY_ROWS, :]

    base = 4 * d_conv
    q_ref[0] = (proj(base, d_sb) * (HEAD_DIM ** -0.5)).astype(bf16)
    k_ref[0] = proj(base + d_sb, d_sb).astype(bf16)
    v_ref[0] = proj(base + 2 * d_sb, d_sb).astype(bf16)
    az_ref[0] = proj(base + 3 * d_sb, d_sb)


def _in_proj(layer, x, norm_g, w_in, conv_w, conv_b, branch_g, d_conv, d_sb):
    bsz, s, d = x.shape
    depth, _, n_in = w_in.shape
    row_spec = lambda width: pl.BlockSpec((1, ROW_TILE, width), lambda b, t: (b, t, 0))
    const = lambda shape: pl.BlockSpec(shape, lambda b, t: (0, 0), pipeline_mode=pl.Buffered(1))
    of_layer = lambda shape: pl.BlockSpec((None,) + shape, lambda b, t: (layer, 0, 0),
                                          pipeline_mode=pl.Buffered(1))
    return pl.pallas_call(
        functools.partial(_in_proj_kernel, layer=layer, d_conv=d_conv, d_sb=d_sb),
        grid=(bsz, s // ROW_TILE),
        in_specs=[row_spec(d), const((depth, d)), of_layer((d, n_in)),
                  of_layer((CONV_WIDTH, d_conv)), const((depth, d_conv)), const((depth, d_conv))],
        out_specs=[row_spec(d_conv), row_spec(d_sb), row_spec(d_sb), row_spec(d_sb), row_spec(d_sb)],
        out_shape=[jax.ShapeDtypeStruct((bsz, s, d_conv), bf16),
                   jax.ShapeDtypeStruct((bsz, s, d_sb), bf16),
                   jax.ShapeDtypeStruct((bsz, s, d_sb), bf16),
                   jax.ShapeDtypeStruct((bsz, s, d_sb), bf16),
                   jax.ShapeDtypeStruct((bsz, s, d_sb), f32)],
        scratch_shapes=[pltpu.VMEM((ROW_TILE + CARRY_ROWS, d_conv), f32)],
        compiler_params=pltpu.CompilerParams(
            dimension_semantics=("arbitrary", "arbitrary"), vmem_limit_bytes=VMEM_LIMIT),
        name="in_proj",
    )(x, norm_g, w_in, conv_w, conv_b, branch_g)


MASKED = -1e30


def _attn_kernel(q_ref, k_ref, v_ref, o_ref, z_scr, d_scr, c_scr, acc_scr):
    t = ATTN_TILE
    i = pl.program_id(2)
    lo_half = lax.broadcasted_iota(jnp.int32, (1, LANES), 1) < HEAD_DIM
    key_pos = lax.broadcasted_iota(jnp.int32, (t, t), 1)
    query_pos = lax.broadcasted_iota(jnp.int32, (t, t), 0)
    causal = key_pos < query_pos
    keys_from = (query_pos >= key_pos).astype(bf16)
    streams = []
    for b in range(q_ref.shape[0]):
        q = q_ref[b]
        zero = jnp.zeros_like(q)
        streams.append((b, jnp.where(lo_half, q, zero)))
        streams.append((b, jnp.where(lo_half, zero, q)))

    def rows_of_tile(m):
        return pl.ds(pl.multiple_of(jnp.maximum(i - m, 0) * t, t), t)

    def scores(n, m):
        b, q_h = streams[n]
        z_scr[n] = lax.dot_general(q_h, k_ref[b, rows_of_tile(m), :], (((1,), (1,)), ((), ())),
                                   preferred_element_type=f32)

    def decay(n, diagonal=False):
        z = z_scr[n]
        c = c_scr[n]
        d_scr[n] = z - jnp.concatenate([c] * (t // LANES), axis=1)
        sp = jnp.maximum(z, 0.0) + jnp.log(1.0 + jnp.exp2(jnp.abs(z) * -LOG2_E))
        if diagonal:
            sp = jnp.where(causal, sp, 0.0)
        suffix = jnp.dot(sp.astype(bf16), keys_from, preferred_element_type=f32)
        d = d_scr[n] - suffix
        if diagonal:
            d = jnp.where(causal, d, MASKED)
        d_scr[n] = d
        c_new = c + suffix[:, 0:1]
        c_scr[n] = c_new
        return c_new

    def weigh(n, m, exists=None):
        a = jnp.exp(d_scr[n]).astype(bf16)
        out = jnp.dot(a, v_ref[streams[n][0], rows_of_tile(m), :], preferred_element_type=f32)
        acc_scr[n] += out if exists is None else jnp.where(exists, out, 0.0)

    def lowest(carries):
        return jnp.min(functools.reduce(jnp.minimum, carries))

    def step(carry):
        m, _ = carry
        c_new = []
        for n in range(len(streams)):
            weigh(n, m)
            c_new.append(decay(n))
            scores(n, m + 2)
        return m + 1, lowest(c_new)

    def more_to_do(carry):
        m, c_low = carry
        return jnp.logical_and(m < i, c_low < DEAD_CARRY)

    c_scr[...] = jnp.zeros(c_scr.shape, f32)
    acc_scr[...] = jnp.zeros(acc_scr.shape, f32)
    for n in range(SCORES_AHEAD):
        scores(n, 0)
    for n in range(len(streams)):
        if n + SCORES_AHEAD < len(streams):
            scores(n + SCORES_AHEAD, 0)
        decay(n, diagonal=True)
        scores(n, 1)
    c_new = []
    for n in range(len(streams)):
        weigh(n, 0)
        c_new.append(decay(n))
    for n in range(len(streams)):
        weigh(n, 1, exists=i >= 1)

    @pl.when(more_to_do((1, lowest(c_new))))
    def _():
        for n in range(len(streams)):
            scores(n, 2)
        c_new = [decay(n) for n in range(len(streams))]
        for n in range(len(streams)):
            scores(n, 3)
        last, _ = lax.while_loop(more_to_do, step, (jnp.int32(2), lowest(c_new)))
        for n in range(len(streams)):
            weigh(n, last)

    for b in range(q_ref.shape[0]):
        o_ref[b] = jnp.where(lo_half, acc_scr[2 * b], acc_scr[2 * b + 1])


def _attention(q, k, v):
    bsz, s, d_sb = q.shape
    pairs = d_sb // LANES
    n_streams = ATTN_BATCH * (LANES // HEAD_DIM)
    tile_spec = pl.BlockSpec((ATTN_BATCH, ATTN_TILE, LANES), lambda b, p, i: (b, i, p))
    seq_spec = pl.BlockSpec((ATTN_BATCH, s, LANES), lambda b, p, i: (b, 0, p))
    return pl.pallas_call(
        _attn_kernel,
        grid=(bsz // ATTN_BATCH, pairs, s // ATTN_TILE),
        in_specs=[tile_spec, seq_spec, seq_spec],
        out_specs=tile_spec,
        out_shape=jax.ShapeDtypeStruct((bsz, s, d_sb), f32),
        scratch_shapes=[pltpu.VMEM((n_streams, ATTN_TILE, ATTN_TILE), f32),
                        pltpu.VMEM((n_streams, ATTN_TILE, ATTN_TILE), f32),
                        pltpu.VMEM((n_streams, ATTN_TILE, LANES), f32),
                        pltpu.VMEM((n_streams, ATTN_TILE, LANES), f32)],
        compiler_params=pltpu.CompilerParams(
            dimension_semantics=("arbitrary", "arbitrary", "arbitrary"),
            vmem_limit_bytes=VMEM_LIMIT),
        name="attention",
    )(q, k, v)


def _out_proj_kernel(x_ref, yc_ref, ya_ref, az_ref, p_ref, bg_ref, wo_ref, pg_ref, wpg_ref, bpg_ref,
                     wpe_ref, fg_ref, o_ref, *, layer, d_conv, final):
    this_layer = slice(layer, layer + 1)
    lo_half = lax.broadcasted_iota(jnp.int32, (1, LANES), 1) < HEAD_DIM

    def attn_branch(rows):
        blocks = []
        for j in range(ya_ref.shape[1] // LANES):
            sl = slice(j * LANES, (j + 1) * LANES)
            y = _pair_group_norm(ya_ref[rows, sl], lo_half) * bg_ref[this_layer, sl]
            blocks.append((y * _silu(az_ref[rows, sl])).astype(bf16))
        return jnp.concatenate(blocks, axis=1)

    halves = [pl.ds(r * (x_ref.shape[0] // 2), x_ref.shape[0] // 2) for r in range(2)]
    x1 = [x_ref[rows, :]
          + jnp.dot(yc_ref[rows, :], wo_ref[0:d_conv, :], preferred_element_type=f32)
          + jnp.dot(attn_branch(rows), wo_ref[d_conv:, :], preferred_element_type=f32)
          for rows in halves]
    for rows, x1_r in zip(halves, x1):
        h = _rmsnorm(x1_r, pg_ref[this_layer, :])
        gate = _sigmoid(jnp.dot(h, wpg_ref[...], preferred_element_type=f32) + bpg_ref[this_layer, :])
        pe = jnp.dot(p_ref[rows, :], wpe_ref[...], preferred_element_type=f32)
        x2 = x1_r + gate * pe
        o_ref[rows, :] = _rmsnorm(x2, fg_ref[...]) if final else x2


def _out_proj(layer, x, y_c, y_a, a_z, p, branch_g, w_out, ple_g, w_pg, b_pg, w_pe, final_g, final):
    n, d = x.shape
    d_conv, d_sb = y_c.shape[1], y_a.shape[1]
    depth, _, ple = p.shape
    assert d_conv % d_sb == 0
    row_spec = lambda width: pl.BlockSpec((ROW_TILE, width), lambda t: (t, 0))
    const = lambda shape: pl.BlockSpec(shape, lambda t: (0, 0), pipeline_mode=pl.Buffered(1))
    of_layer = lambda shape: pl.BlockSpec((None,) + shape, lambda t: (layer, 0, 0),
                                          pipeline_mode=pl.Buffered(1))
    return pl.pallas_call(
        functools.partial(_out_proj_kernel, layer=layer, d_conv=d_conv, final=final),
        grid=(n // ROW_TILE,),
        in_specs=[row_spec(d), row_spec(d_conv), row_spec(d_sb), row_spec(d_sb),
                  pl.BlockSpec((None, ROW_TILE, ple), lambda t: (layer, t, 0)),
                  pl.BlockSpec((depth, d_sb), lambda t: (0, d_conv // d_sb),
                               pipeline_mode=pl.Buffered(1)),
                  of_layer((d_conv + d_sb, d)), const((depth, d)), of_layer((d, d)), const((depth, d)),
                  of_layer((ple, d)), const((1, d))],
        out_specs=row_spec(d),
        out_shape=jax.ShapeDtypeStruct((n, d), f32),
        compiler_params=pltpu.CompilerParams(
            dimension_semantics=("arbitrary",), vmem_limit_bytes=VMEM_LIMIT),
        name="out_proj",
    )(x, y_c, y_a, a_z, p, branch_g, w_out, ple_g, w_pg, b_pg, w_pe, final_g)


def kernel(x, p, norm_g, w_in, conv_w, conv_b, branch_g, w_out, ple_norm_g, w_pg, b_pg, w_pe, final_g):
    depth = w_in.shape[0]
    bsz, s, d = x.shape
    d_conv = conv_w.shape[2]
    d_sb = branch_g.shape[1] - d_conv
    assert s % ROW_TILE == 0 and s % ATTN_TILE == 0 and d_sb % LANES == 0 and d_conv % LANES == 0
    assert bsz % ATTN_BATCH == 0
    w_out = w_out.astype(bf16)
    p = p.reshape(depth, bsz * s, -1)
    final_g = final_g.reshape(1, -1)
    for i in range(depth):
        y_c, q, k, v, a_z = _in_proj(i, x, norm_g, w_in, conv_w, conv_b, branch_g, d_conv, d_sb)
        y_a = _attention(q, k, v)
        x = _out_proj(
            i, x.reshape(bsz * s, d), y_c.reshape(bsz * s, d_conv), y_a.reshape(bsz * s, d_sb),
            a_z.reshape(bsz * s, d_sb), p, branch_g, w_out, ple_norm_g, w_pg, b_pg, w_pe, final_g,
            final=(i == depth - 1)).reshape(bsz, s, d)
    return x
```
